```python
import jax, jax.numpy as jnp
from jax import lax
import numpy as np

D_MODEL = 2048
BATCH = 4
SEQ = 2048
DEPTH = 4

GRID_W = 64
CTX_LEN = 256
NH_M = 4
DH_M = 256
D_M = NH_M * DH_M
NG_F = 4
DG_F = 128
D_F = NG_F * DG_F
D_C = 512
D_FF = 5632
CHUNK = 128
CONV_W = 3
EPS = 1e-6
N_BRANCH = 3

OFF_GATES = 2 * D_M
N_STATE_COLS = 2 * D_M + 4 * NH_M
OFF_Q = N_STATE_COLS
OFF_O = OFF_Q + D_M
OFF_F = OFF_O + D_M
OFF_C = OFF_F + D_F
OFF_G = OFF_C + 3 * D_C
N_IN = OFF_G + N_BRANCH * D_MODEL

kernel_name = "hybrid_mlstm_fourier_shortconv_dit_block"


def rmsnorm(x, w):
    xf = x.astype(jnp.float32)
    y = xf * lax.rsqrt(jnp.mean(xf * xf, axis=-1, keepdims=True) + EPS)
    return (y * w.astype(jnp.float32)).astype(x.dtype)


def modulation(cond, w_mod, b_mod):
    m = (jax.nn.silu(cond) @ w_mod + b_mod).reshape(-1, 1, 6 * D_MODEL)
    return jnp.split(m, 6, axis=-1)


def modulate(x, w, shift, scale):
    return rmsnorm(x, w) * (1.0 + scale) + shift


def dwconv3(x, w, axis):
    n = x.shape[axis]
    half = CONV_W // 2
    pad = [(0, 0)] * x.ndim
    pad[axis] = (half, half)
    xp = jnp.pad(x, pad)
    out = lax.slice_in_dim(xp, 0, n, axis=axis) * w[0]
    for j in range(1, CONV_W):
        out = out + lax.slice_in_dim(xp, j, j + n, axis=axis) * w[j]
    return out


def conv_mix(x, w, grid, grid_axis):
    if grid is None:
        return dwconv3(x, w, 1)
    B, T, C = x.shape
    return dwconv3(x.reshape(B, grid[0], grid[1], C), w, grid_axis).reshape(B, T, C)


def mlstm_scan(q, k, v, i_pre, f_pre, state):
    B, T, H, Dh = k.shape
    nc = T // CHUNK

    def chunks(a):
        a = a.astype(jnp.float32).reshape((B, nc, CHUNK) + a.shape[2:])
        return jnp.swapaxes(jnp.moveaxis(a, 1, 0), 2, 3)

    with_h = q is not None
    xs = (chunks(k), chunks(v), chunks(i_pre), chunks(jax.nn.log_sigmoid(f_pre.astype(jnp.float32))))
    if with_h:
        xs = xs + (chunks(q),)
    causal = jnp.tril(jnp.ones((CHUNK, CHUNK), dtype=bool))

    def step(carry, inp):
        C, n, m = carry
        kc, vc, ic, lfc = inp[:4]
        b = jnp.cumsum(lfc, axis=-1)
        b_end = b[..., -1]
        a_end = b_end[..., None] - b + ic
        m_new = jnp.maximum(b_end + m, jnp.max(a_end, axis=-1))
        w_end = jnp.exp(a_end - m_new[..., None])
        keep = jnp.exp(b_end + m - m_new)
        C_new = keep[..., None, None] * C + jnp.einsum("bhsk,bhsv->bhkv", kc * w_end[..., None], vc)
        n_new = keep[..., None] * n + jnp.einsum("bhsk,bhs->bhk", kc, w_end)
        if not with_h:
            return (C_new, n_new, m_new), None
        qc = inp[4]
        log_d = jnp.where(causal, b[..., :, None] - b[..., None, :] + ic[..., None, :], -jnp.inf)
        inter = b + m[..., None]
        m_t = jnp.maximum(inter, jnp.max(log_d, axis=-1))
        decay = jnp.exp(log_d - m_t[..., None])
        g_inter = jnp.exp(inter - m_t)
        s = jnp.einsum("bhtk,bhsk->bhts", qc, kc) * decay
        num = g_inter[..., None] * jnp.einsum("bhtk,bhkv->bhtv", qc, C) + jnp.einsum("bhts,bhsv->bhtv", s, vc)
        den = g_inter * jnp.einsum("bhtk,bhk->bht", qc, n) + jnp.sum(s, axis=-1)
        h = num / jnp.maximum(jnp.abs(den), jnp.exp(-m_t))[..., None]
        return (C_new, n_new, m_new), h

    state, h = lax.scan(step, state, xs)
    if with_h:
        h = jnp.transpose(h, (1, 0, 3, 2, 4)).reshape(B, T, H * Dh)
    return h, state


def bidir_mlstm(q, k, v, gates, init_states):
    B, T, H, Dh = k.shape
    if init_states is None:
        zero = (jnp.zeros((B, H, Dh, Dh), jnp.float32), jnp.zeros((B, H, Dh), jnp.float32),
                jnp.zeros((B, H), jnp.float32))
        init_states = (zero, zero)
    rev = lambda a: None if a is None else jnp.flip(a, axis=1)
    h_f, st_f = mlstm_scan(q, k, v, gates[:, :, 0], gates[:, :, 1], init_states[0])
    h_b, st_b = mlstm_scan(rev(q), rev(k), rev(v), rev(gates[:, :, 2]), rev(gates[:, :, 3]), init_states[1])
    h = None if q is None else h_f + rev(h_b)
    return h, (st_f, st_b)


def mlstm_inputs(p_state, conv_k_w):
    B, T, _ = p_state.shape
    k = jax.nn.silu(dwconv3(p_state[..., :D_M], conv_k_w, 1)).reshape(B, T, NH_M, DH_M)
    v = p_state[..., D_M:2 * D_M].reshape(B, T, NH_M, DH_M)
    gates = p_state[..., OFF_GATES:N_STATE_COLS].reshape(B, T, 4, NH_M)
    return k, v, gates


def fourier_mix(xf):
    B, T, _ = xf.shape
    a = xf.astype(jnp.float32).reshape(B, T, NG_F, DG_F)
    y = jnp.fft.fft2(a, axes=(1, 3), norm="ortho").real
    return y.reshape(B, T, D_F).astype(xf.dtype)


def mixer(xn, p, grid, init_states):
    B, T, _ = xn.shape
    proj = xn @ p["w_in"] + p["b_in"]
    k, v, gates = mlstm_inputs(proj[..., :N_STATE_COLS], p["conv_k_w"])
    q = jax.nn.silu(dwconv3(proj[..., OFF_Q:OFF_O], p["conv_q_w"], 1)).reshape(B, T, NH_M, DH_M) * (DH_M ** -0.5)
    o_gate = proj[..., OFF_O:OFF_F]
    xf = proj[..., OFF_F:OFF_C]
    cb = proj[..., OFF_C:OFF_C + D_C]
    cc = proj[..., OFF_C + D_C:OFF_C + 2 * D_C]
    cx = proj[..., OFF_C + 2 * D_C:OFF_G]
    gm = proj[..., OFF_G:]

    h, states = bidir_mlstm(q, k, v, gates, init_states)
    h = rmsnorm(h.astype(xn.dtype).reshape(B, T, NH_M, DH_M), p["mlstm_norm_w"].reshape(NH_M, DH_M)).reshape(B, T, D_M)
    y_m = (jax.nn.sigmoid(o_gate) * h) @ p["w_pm"]
    y_f = fourier_mix(xf) @ p["w_pf"]
    y_c = (cb * conv_mix(cc * cx, p["conv_c_w"], grid, 2)) @ p["w_pc"]
    g = jax.nn.sigmoid(gm).reshape(B, T, N_BRANCH, D_MODEL)
    merged = g[..., 0, :] * y_m + g[..., 1, :] * y_f + g[..., 2, :] * y_c
    return merged @ p["w_o"], states


def ctx_states(xn_c, p):
    p_state = xn_c @ p["w_in"][:, :N_STATE_COLS] + p["b_in"][:N_STATE_COLS]
    k, v, gates = mlstm_inputs(p_state, p["conv_k_w"])
    _, states = bidir_mlstm(None, k, v, gates, None)
    return states


def conv_ffn(xn, p, grid):
    u = xn @ p["w_up"]
    a = conv_mix(u[..., :D_FF], p["conv_ff_w"], grid, 1)
    return (jax.nn.silu(a) * u[..., D_FF:]) @ p["w_down"]


def setup_inputs(seed: int = 0) -> dict:
    key = jax.random.key(seed)
    ks = jax.random.split(key, 24)
    D = D_MODEL
    nrm = lambda k, shape, scale: scale * jax.random.normal(k, shape, jnp.float32)
    b_in = nrm(ks[9], (DEPTH, N_IN), 0.02)
    b_in = b_in.at[:, OFF_GATES + NH_M:OFF_GATES + 2 * NH_M].add(3.0)
    b_in = b_in.at[:, OFF_GATES + 3 * NH_M:OFF_GATES + 4 * NH_M].add(3.0)
    return {
        "x": nrm(ks[0], (BATCH, SEQ, D), 1.0),
        "c": nrm(ks[1], (BATCH, D), 1.0),
        "ctx": nrm(ks[2], (BATCH, CTX_LEN, D), 1.0),
        "c_ctx": nrm(ks[3], (D,), 1.0),
        "w_mod": nrm(ks[4], (DEPTH, D, 6 * D), 0.5 * D ** -0.5),
        "b_mod": nrm(ks[5], (DEPTH, 6 * D), 0.02),
        "norm1_w": 1.0 + nrm(ks[6], (DEPTH, D), 0.02),
        "norm2_w": 1.0 + nrm(ks[7], (DEPTH, D), 0.02),
        "w_in": nrm(ks[8], (DEPTH, D, N_IN), D ** -0.5),
        "b_in": b_in,
        "conv_q_w": nrm(ks[10], (DEPTH, CONV_W, D_M), CONV_W ** -0.5),
        "conv_k_w": nrm(ks[11], (DEPTH, CONV_W, D_M), CONV_W ** -0.5),
        "mlstm_norm_w": 1.0 + nrm(ks[12], (DEPTH, D_M), 0.02),
        "w_pm": nrm(ks[13], (DEPTH, D_M, D), D_M ** -0.5),
        "w_pf": nrm(ks[14], (DEPTH, D_F, D), D_F ** -0.5),
        "w_pc": nrm(ks[15], (DEPTH, D_C, D), D_C ** -0.5),
        "conv_c_w": nrm(ks[16], (DEPTH, CONV_W, D_C), CONV_W ** -0.5),
        "w_o": nrm(ks[17], (DEPTH, D, D), D ** -0.5),
        "w_up": nrm(ks[18], (DEPTH, D, 2 * D_FF), D ** -0.5),
        "conv_ff_w": nrm(ks[19], (DEPTH, CONV_W, D_FF), CONV_W ** -0.5),
        "w_down": nrm(ks[20], (DEPTH, D_FF, D), D_FF ** -0.5),
        "final_norm_w": 1.0 + nrm(ks[21], (D,), 0.02),
    }


def reference(x, c, ctx, c_ctx, w_mod, b_mod, norm1_w, norm2_w, w_in, b_in, conv_q_w, conv_k_w,
              mlstm_norm_w, w_pm, w_pf, w_pc, conv_c_w, w_o, w_up, conv_ff_w, w_down, final_norm_w):
    rows = x.shape[1] // GRID_W
    grid = (rows, GRID_W)
    h, hc = x, ctx
    for l in range(DEPTH):
        p = {"w_in": w_in[l], "b_in": b_in[l], "conv_q_w": conv_q_w[l], "conv_k_w": conv_k_w[l],
             "mlstm_norm_w": mlstm_norm_w[l], "w_pm": w_pm[l], "w_pf": w_pf[l], "w_pc": w_pc[l],
             "conv_c_w": conv_c_w[l], "w_o": w_o[l], "w_up": w_up[l], "conv_ff_w": conv_ff_w[l],
             "w_down": w_down[l]}
        sh1, sc1, g1, sh2, sc2, g2 = modulation(c, w_mod[l], b_mod[l])
        csh1, csc1, cg1, csh2, csc2, cg2 = modulation(c_ctx, w_mod[l], b_mod[l])
        xn_c = modulate(hc, norm1_w[l], csh1, csc1)
        if l == DEPTH - 1:
            states = ctx_states(xn_c, p)
        else:
            out_c, states = mixer(xn_c, p, None, None)
            hc = hc + cg1 * out_c
            hc = hc + cg2 * conv_ffn(modulate(hc, norm2_w[l], csh2, csc2), p, None)
        out, _ = mixer(modulate(h, norm1_w[l], sh1, sc1), p, grid, states)
        h = h + g1 * out
        h = h + g2 * conv_ffn(modulate(h, norm2_w[l], sh2, sc2), p, grid)
    return rmsnorm(h, final_norm_w)
```

```python
import functools

import jax
import jax.numpy as jnp
from jax import lax
from jax.experimental import pallas as pl
from jax.experimental.pallas import tpu as pltpu

f32 = jnp.float32
bf16 = jnp.bfloat16

D = 2048
B = 4
T = 2048
T_CTX = 256
DEPTH = 4
GRID_W = 64
NH = 4
DH = 256
DM = NH * DH
NG = 4
DG = 128
DF = NG * DG
DC = 512
DFF = 5632
CHUNK = 128
EPS = 1e-6

R_LAT = B * T
R_CTX = B * T_CTX
R = R_LAT + R_CTX

COL_K, COL_V, COL_Q, COL_O = 0, DM, 2 * DM, 3 * DM
COL_F = 4 * DM
COL_CB = COL_F + DF
COL_CC = COL_CB + DC
COL_CX = COL_CC + DC
COL_G = COL_CX + DC
N_PROJ = COL_G + 3 * D
REF_GATES = 2 * DM
REF_Q = REF_GATES + 4 * NH
GATE_PAD = 128

VMEM_LIMIT = 56 * 1024 * 1024


def _cparams(*sem):
    return pltpu.CompilerParams(dimension_semantics=sem, vmem_limit_bytes=VMEM_LIMIT)


def _mod_row(row_start):
    return jnp.where(row_start < R_LAT, row_start // T, B)


def _mod_spec(comp, tm):
    return pl.BlockSpec((None, None, 1, D), lambda i, *_: (_mod_row(i * tm), comp, 0, 0))


def _sigmoid(x):
    return jax.nn.sigmoid(x)


def _mod_kernel(c_ref, w_ref, b_ref, o_ref):
    c = c_ref[...]
    s = (c * _sigmoid(c)).astype(bf16)
    o_ref[...] = jnp.dot(s, w_ref[...].astype(bf16), preferred_element_type=f32) + b_ref[...]


def _modulation(cc8, w_mod, b_mod):
    tn = 1024
    n = 6 * D
    return pl.pallas_call(
        _mod_kernel,
        grid=(DEPTH, n // tn),
        in_specs=[
            pl.BlockSpec((8, D), lambda l, j: (0, 0)),
            pl.BlockSpec((None, D, tn), lambda l, j: (l, 0, j)),
            pl.BlockSpec((None, 1, tn), lambda l, j: (l, 0, j)),
        ],
        out_specs=pl.BlockSpec((None, 8, tn), lambda l, j: (l, 0, j)),
        out_shape=jax.ShapeDtypeStruct((DEPTH, 8, n), f32),
        compiler_params=_cparams("arbitrary", "arbitrary"),
        name="modulation",
    )(cc8, w_mod, b_mod.reshape(DEPTH, 1, n))


NORM_ROWS = 128


def _norm_matmul_kernel(h_ref, nw_ref, sh_ref, sc_ref, w_ref, b_ref, o_ref, xn_ref):
    @pl.when(pl.program_id(1) == 0)
    def _():
        for r in range(0, h_ref.shape[0], NORM_ROWS):
            x = h_ref[r:r + NORM_ROWS, :]
            ms = jnp.mean(x * x, axis=-1, keepdims=True)
            y = x * lax.rsqrt(ms + EPS) * nw_ref[...]
            xn_ref[r:r + NORM_ROWS, :] = (y * (1.0 + sc_ref[...]) + sh_ref[...]).astype(bf16)

    acc = jnp.dot(xn_ref[...], w_ref[...], preferred_element_type=f32)
    o_ref[...] = (acc + b_ref[...]).astype(o_ref.dtype)


def _norm_matmul(h, norm_w, mods_l, comp_shift, comp_scale, w, bias, out_dtype, tm, tn, name):
    n = w.shape[1]
    return pl.pallas_call(
        _norm_matmul_kernel,
        grid=(R // tm, n // tn),
        in_specs=[
            pl.BlockSpec((tm, D), lambda i, j: (i, 0)),
            pl.BlockSpec((1, D), lambda i, j: (0, 0)),
            _mod_spec(comp_shift, tm),
            _mod_spec(comp_scale, tm),
            pl.BlockSpec((D, tn), lambda i, j: (0, j)),
            pl.BlockSpec((1, tn), lambda i, j: (0, j)),
        ],
        out_specs=pl.BlockSpec((tm, tn), lambda i, j: (i, j)),
        out_shape=jax.ShapeDtypeStruct((R, n), out_dtype),
        scratch_shapes=[pltpu.VMEM((tm, D), bf16)],
        compiler_params=_cparams("arbitrary", "arbitrary"),
        name=name,
    )(h, norm_w.reshape(1, D), mods_l, mods_l, w, bias.reshape(1, n))


QK_TR = 256
QK_TC = 512
HALO = 16


def _qk_conv_kernel(km, kp, kn, qm, qp, qn, wk_ref, wq_ref, ko_ref, qo_ref):
    r0 = pl.program_id(0) * QK_TR
    is_ctx = r0 >= R_LAT
    is_start = jnp.logical_or(is_ctx, r0 % T == 0)
    is_end = jnp.logical_or(is_ctx, (r0 + QK_TR) % T == 0)
    row = lax.broadcasted_iota(jnp.int32, (QK_TR, 1), 0)

    def conv_silu(m_ref, p_ref, n_ref, w_ref):
        x = m_ref[...].astype(f32)
        pv = jnp.where(is_start, 0.0, p_ref[...].astype(f32)[HALO - 1:HALO, :])
        nv = jnp.where(is_end, 0.0, n_ref[...].astype(f32)[0:1, :])
        x_prev = jnp.where(row == 0, pv, pltpu.roll(x, 1, 0))
        x_next = jnp.where(row == QK_TR - 1, nv, pltpu.roll(x, QK_TR - 1, 0))
        w = w_ref[...]
        a = x_prev * w[0:1, :] + x * w[1:2, :] + x_next * w[2:3, :]
        return a * _sigmoid(a)

    ko_ref[...] = conv_silu(km, kp, kn, wk_ref).astype(bf16)
    qo_ref[...] = (conv_silu(qm, qp, qn, wq_ref) * (DH ** -0.5)).astype(bf16)


def _qk_conv(proj, conv_k_w, conv_q_w):
    per = QK_TR // HALO
    nblk = R // HALO

    def main(col0):
        return pl.BlockSpec((QK_TR, QK_TC), lambda i, j: (i, col0 // QK_TC + j))

    def prev(col0):
        return pl.BlockSpec((HALO, QK_TC), lambda i, j: (jnp.maximum(i * per - 1, 0), col0 // QK_TC + j))

    def nxt(col0):
        return pl.BlockSpec((HALO, QK_TC), lambda i, j: (jnp.minimum((i + 1) * per, nblk - 1), col0 // QK_TC + j))

    wspec = pl.BlockSpec((3, QK_TC), lambda i, j: (0, j))
    ospec = pl.BlockSpec((QK_TR, QK_TC), lambda i, j: (i, j))
    return pl.pallas_call(
        _qk_conv_kernel,
        grid=(R // QK_TR, DM // QK_TC),
        in_specs=[main(COL_K), prev(COL_K), nxt(COL_K), main(COL_Q), prev(COL_Q), nxt(COL_Q), wspec, wspec],
        out_specs=[ospec, ospec],
        out_shape=[jax.ShapeDtypeStruct((R, DM), bf16)] * 2,
        compiler_params=_cparams("arbitrary", "arbitrary"),
        name="qk_conv",
    )(proj, proj, proj, proj, proj, proj, conv_k_w, conv_q_w)


def _log_sigmoid(x):
    return jnp.minimum(x, 0.0) - jnp.log1p(jnp.exp(-jnp.abs(x)))


def _mlstm_chunk(q_ref, k_ref, v_ref, gc_ref, gr_ref, c, reverse, carry, c_ref):
    n, m = carry
    rows = pl.ds(pl.multiple_of(c * CHUNK, CHUNK), CHUNK)
    q = q_ref[rows, :]
    k = k_ref[rows, :]
    v = v_ref[rows, :]
    gcol = gc_ref[rows, :]
    grow = gr_ref[:, rows]
    gi, gf = (2, 3) if reverse else (0, 1)
    i_col = gcol[:, gi:gi + 1]
    lf_col = _log_sigmoid(gcol[:, gf:gf + 1])
    i_row = grow[gi:gi + 1, :]
    lf_row = _log_sigmoid(grow[gf:gf + 1, :])

    t_idx = lax.broadcasted_iota(jnp.int32, (CHUNK, CHUNK), 0)
    s_idx = lax.broadcasted_iota(jnp.int32, (CHUNK, CHUNK), 1)
    seen = (s_idx >= t_idx) if reverse else (s_idx <= t_idx)
    seen_t = (t_idx >= s_idx) if reverse else (t_idx <= s_idx)
    b_col = jnp.sum(jnp.where(seen, lf_row, 0.0), axis=1, keepdims=True)
    b_row = jnp.sum(jnp.where(seen_t, lf_col, 0.0), axis=0, keepdims=True)
    b_end = jnp.sum(lf_row, axis=1, keepdims=True)

    a_end = b_end - b_col + i_col
    m_new = jnp.maximum(b_end + m, jnp.max(a_end, axis=0, keepdims=True))
    w_end = jnp.exp(a_end - m_new)
    keep = jnp.exp(b_end + m - m_new)

    log_d = jnp.where(seen, b_col - b_row + i_row, -jnp.inf)
    inter = b_col + m
    m_t = jnp.maximum(inter, jnp.max(log_d, axis=1, keepdims=True))
    decay = jnp.exp(log_d - m_t)
    g_inter = jnp.exp(inter - m_t)

    s = lax.dot_general(q, k, (((1,), (1,)), ((), ())), preferred_element_type=f32) * decay
    c_old = c_ref[...]
    num = g_inter * jnp.dot(q, c_old.astype(bf16), preferred_element_type=f32)
    num = num + jnp.dot(s.astype(bf16), v, preferred_element_type=f32)
    qn = jnp.sum(q.astype(f32) * n, axis=1, keepdims=True)
    den = g_inter * qn + jnp.sum(s, axis=1, keepdims=True)
    h = num / jnp.maximum(jnp.abs(den), jnp.exp(-m_t))

    kw = k.astype(f32) * w_end
    kv = lax.dot_general(kw.astype(bf16), v, (((0,), (0,)), ((), ())), preferred_element_type=f32)
    c_ref[...] = keep * c_old + kv
    n_new = keep * n + jnp.sum(kw, axis=0, keepdims=True)
    return h, (n_new, m_new)


def _mlstm_kernel(ql, qc, kl, kc, vl, vc, gcl, gcc, grl, grc, ol, oc, nw_ref,
                  outl, outc, hfl, hfc, c_ref):
    zero = (jnp.zeros((1, DH), f32), jnp.zeros((1, 1), f32))
    lat = (ql, kl, vl, gcl, grl)
    ctx = (qc, kc, vc, gcc, grc)
    n_lat = T // CHUNK
    n_ctx = T_CTX // CHUNK

    def fwd(refs, hf_ref, count, carry):
        def body(c, carry):
            h, carry = _mlstm_chunk(*refs, c, False, carry, c_ref)
            hf_ref[pl.ds(pl.multiple_of(c * CHUNK, CHUNK), CHUNK), :] = h
            return carry
        return lax.fori_loop(0, count, body, carry)

    def bwd(refs, hf_ref, o_ref, out_ref, count, carry):
        def body(j, carry):
            c = count - 1 - j
            h, carry = _mlstm_chunk(*refs, c, True, carry, c_ref)
            rows = pl.ds(pl.multiple_of(c * CHUNK, CHUNK), CHUNK)
            ht = hf_ref[rows, :] + h
            y = ht * lax.rsqrt(jnp.mean(ht * ht, axis=-1, keepdims=True) + EPS) * nw_ref[...]
            out_ref[rows, :] = (_sigmoid(o_ref[rows, :].astype(f32)) * y).astype(bf16)
            return carry
        return lax.fori_loop(0, count, body, carry)

    c_ref[...] = jnp.zeros((DH, DH), f32)
    carry = fwd(ctx, hfc, n_ctx, zero)
    fwd(lat, hfl, n_lat, carry)
    c_ref[...] = jnp.zeros((DH, DH), f32)
    carry = bwd(ctx, hfc, oc, outc, n_ctx, zero)
    bwd(lat, hfl, ol, outl, n_lat, carry)


def _mlstm(q, k, proj, gcol, grow, norm_w):
    ctx_blk = R_LAT // T_CTX

    def lat(col0):
        return pl.BlockSpec((T, DH), lambda b, h: (b, col0 // DH + h))

    def ctx(col0):
        return pl.BlockSpec((T_CTX, DH), lambda b, h: (ctx_blk + b, col0 // DH + h))

    in_specs = [
        lat(0), ctx(0), lat(0), ctx(0), lat(COL_V), ctx(COL_V),
        pl.BlockSpec((None, T, 4), lambda b, h: (h, b, 0)),
        pl.BlockSpec((None, T_CTX, 4), lambda b, h: (h, ctx_blk + b, 0)),
        pl.BlockSpec((None, 4, T), lambda b, h: (h, 0, b)),
        pl.BlockSpec((None, 4, T_CTX), lambda b, h: (h, 0, ctx_blk + b)),
        lat(COL_O), ctx(COL_O),
        pl.BlockSpec((1, DH), lambda b, h: (0, h)),
    ]
    out_l, out_c = pl.pallas_call(
        _mlstm_kernel,
        grid=(B, NH),
        in_specs=in_specs,
        out_specs=[pl.BlockSpec((T, DH), lambda b, h: (b, h)), pl.BlockSpec((T_CTX, DH), lambda b, h: (b, h))],
        out_shape=[jax.ShapeDtypeStruct((R_LAT, DM), bf16), jax.ShapeDtypeStruct((R_CTX, DM), bf16)],
        scratch_shapes=[pltpu.VMEM((T, DH), f32), pltpu.VMEM((T_CTX, DH), f32), pltpu.VMEM((DH, DH), f32)],
        compiler_params=_cparams("arbitrary", "arbitrary"),
        name="mlstm",
    )(q, q, k, k, proj, proj, gcol, gcol, grow, grow, proj, proj, norm_w.reshape(1, DM))
    return jnp.concatenate([out_l, out_c], axis=0)


def _dft_tables(t_len):
    def cs(n):
        i = lax.broadcasted_iota(jnp.int32, (n, n), 0)
        j = lax.broadcasted_iota(jnp.int32, (n, n), 1)
        ang = ((i * j) % n).astype(f32) * (2.0 * jnp.pi / n)
        return jnp.cos(ang), jnp.sin(ang)

    ct, st = cs(t_len)
    cg, sg = cs(DG)
    scale = (t_len * DG) ** -0.5
    return (jnp.concatenate([ct, st], axis=1).astype(bf16),
            (jnp.concatenate([cg, -sg], axis=1) * scale).astype(bf16))


def _fourier_kernel(x_ref, cs_ref, f_ref, o_ref, z_ref, *, t_len):
    x = x_ref[...]
    for g in range(NG):
        cols = slice(g * DG, (g + 1) * DG)
        pq = jnp.dot(x[:, cols], cs_ref[...], preferred_element_type=f32)
        z_ref[0:t_len, cols] = pq[:, :DG].astype(bf16)
        z_ref[t_len:2 * t_len, cols] = pq[:, DG:].astype(bf16)
    o_ref[...] = jnp.dot(f_ref[...], z_ref[...], preferred_element_type=f32).astype(bf16)


def _fourier_call(proj, t_len, row_blk0, prev_out):
    f_tab, g_tab = _dft_tables(t_len)
    kern = functools.partial(_fourier_kernel, t_len=t_len)
    in_specs = [
        pl.BlockSpec((t_len, DF), lambda b: (row_blk0 + b, COL_F // DF)),
        pl.BlockSpec((DG, 2 * DG), lambda b: (0, 0)),
        pl.BlockSpec((t_len, 2 * t_len), lambda b: (0, 0), pipeline_mode=pl.Buffered(1)),
    ]
    args = [proj, g_tab, f_tab]
    aliases = {}
    if prev_out is not None:
        in_specs.append(pl.BlockSpec(memory_space=pl.ANY))
        args.append(prev_out)
        aliases = {3: 0}

        def body(x_ref, cs_ref, f_ref, _, o_ref, z_ref):
            kern(x_ref, cs_ref, f_ref, o_ref, z_ref)
    else:
        body = kern
    return pl.pallas_call(
        body,
        grid=(B,),
        in_specs=in_specs,
        out_specs=pl.BlockSpec((t_len, DF), lambda b: (row_blk0 + b, 0)),
        out_shape=jax.ShapeDtypeStruct((R, DF), bf16),
        scratch_shapes=[pltpu.VMEM((2 * t_len, DF), bf16)],
        input_output_aliases=aliases,
        compiler_params=_cparams("arbitrary"),
        name=f"fourier_{t_len}",
    )(*args)


def _fourier(proj):
    y = _fourier_call(proj, T, 0, None)
    return _fourier_call(proj, T_CTX, R_LAT // T_CTX, y)


MG_TM = 256


def _merge_kernel(hm_ref, yf_ref, cb_ref, cc_ref, cx_ref, wc_ref, gm_ref, wpm_ref, wpf_ref, wpc_ref, wo_ref,
                  h_ref, g1_ref, o_ref):
    r0 = pl.program_id(0) * MG_TM
    period = jnp.where(r0 >= R_LAT, T_CTX, GRID_W)
    row = lax.broadcasted_iota(jnp.int32, (MG_TM, 1), 0)
    pos = row % period
    ccx = cc_ref[...].astype(f32) * cx_ref[...].astype(f32)
    x_prev = jnp.where(pos == 0, 0.0, pltpu.roll(ccx, 1, 0))
    x_next = jnp.where(pos == period - 1, 0.0, pltpu.roll(ccx, MG_TM - 1, 0))
    wc = wc_ref[...]
    uc = cb_ref[...].astype(f32) * (x_prev * wc[0:1, :] + ccx * wc[1:2, :] + x_next * wc[2:3, :])

    y_m = jnp.dot(hm_ref[...], wpm_ref[...], preferred_element_type=f32)
    y_f = jnp.dot(yf_ref[...], wpf_ref[...], preferred_element_type=f32)
    y_c = jnp.dot(uc.astype(bf16), wpc_ref[...], preferred_element_type=f32)
    merged = _sigmoid(gm_ref[:, 0:D].astype(f32)) * y_m
    merged = merged + _sigmoid(gm_ref[:, D:2 * D].astype(f32)) * y_f
    merged = merged + _sigmoid(gm_ref[:, 2 * D:3 * D].astype(f32)) * y_c
    out = jnp.dot(merged.astype(bf16), wo_ref[...], preferred_element_type=f32)
    o_ref[...] = h_ref[...] + g1_ref[...] * out


def _merge(hm, yf, proj, conv_c_w, w_pm, w_pf, w_pc, w_o, h, mods_l):
    tm = MG_TM

    def const(shape):
        return pl.BlockSpec(shape, lambda i: (0, 0), pipeline_mode=pl.Buffered(1))

    return pl.pallas_call(
        _merge_kernel,
        grid=(R // tm,),
        in_specs=[
            pl.BlockSpec((tm, DM), lambda i: (i, 0)),
            pl.BlockSpec((tm, DF), lambda i: (i, 0)),
            pl.BlockSpec((tm, DC), lambda i: (i, COL_CB // DC)),
            pl.BlockSpec((tm, DC), lambda i: (i, COL_CC // DC)),
            pl.BlockSpec((tm, DC), lambda i: (i, COL_CX // DC)),
            pl.BlockSpec((3, DC), lambda i: (0, 0)),
            pl.BlockSpec((tm, 3 * D), lambda i: (i, COL_G // (3 * D))),
            const((DM, D)), const((DF, D)), const((DC, D)), const((D, D)),
            pl.BlockSpec((tm, D), lambda i: (i, 0)),
            _mod_spec(2, tm),
        ],
        out_specs=pl.BlockSpec((tm, D), lambda i: (i, 0)),
        out_shape=jax.ShapeDtypeStruct((R, D), f32),
        compiler_params=_cparams("arbitrary"),
        name="merge",
    )(hm, yf, proj, proj, proj, conv_c_w, proj, w_pm, w_pf, w_pc, w_o, h, mods_l)


FF_TM = 256
FF_TC = 512


def _ffn_down_kernel(ua_ref, up_ref, un_ref, ub_ref, wf_ref, wd_ref, h_ref, g2_ref, o_ref, act_ref):
    r0 = pl.program_id(0) * FF_TM
    is_ctx = r0 >= R_LAT

    def finish(cols, a_prev, a, a_next):
        wf = wf_ref[:, cols]
        x = a_prev * wf[0:1, :] + a * wf[1:2, :] + a_next * wf[2:3, :]
        act_ref[:, cols] = (x * _sigmoid(x) * ub_ref[:, cols].astype(f32)).astype(bf16)

    @pl.when(is_ctx)
    def _():
        row = lax.broadcasted_iota(jnp.int32, (FF_TM, 1), 0)
        pos = row % T_CTX
        for c0 in range(0, DFF, FF_TC):
            cols = slice(c0, c0 + FF_TC)
            a = ua_ref[:, cols].astype(f32)
            a_prev = jnp.where(pos == 0, 0.0, pltpu.roll(a, 1, 0))
            a_next = jnp.where(pos == T_CTX - 1, 0.0, pltpu.roll(a, FF_TM - 1, 0))
            finish(cols, a_prev, a, a_next)

    @pl.when(jnp.logical_not(is_ctx))
    def _():
        for c0 in range(0, DFF, FF_TC):
            cols = slice(c0, c0 + FF_TC)
            a = ua_ref[:, cols].astype(f32)
            top = jnp.where(r0 % T == 0, 0.0, up_ref[:, cols].astype(f32))
            bot = jnp.where((r0 + FF_TM) % T == 0, 0.0, un_ref[:, cols].astype(f32))
            a_prev = jnp.concatenate([top, a[:FF_TM - GRID_W, :]], axis=0)
            a_next = jnp.concatenate([a[GRID_W:, :], bot], axis=0)
            finish(cols, a_prev, a, a_next)

    out = jnp.dot(act_ref[...], wd_ref[...], preferred_element_type=f32)
    o_ref[...] = h_ref[...] + g2_ref[...] * out


def _ffn_down(u, conv_ff_w, w_down, h, mods_l):
    tm = FF_TM
    per = tm // GRID_W
    nblk = R // GRID_W
    return pl.pallas_call(
        _ffn_down_kernel,
        grid=(R // tm,),
        in_specs=[
            pl.BlockSpec((tm, DFF), lambda i: (i, 0)),
            pl.BlockSpec((GRID_W, DFF), lambda i: (jnp.maximum(i * per - 1, 0), 0)),
            pl.BlockSpec((GRID_W, DFF), lambda i: (jnp.minimum((i + 1) * per, nblk - 1), 0)),
            pl.BlockSpec((tm, DFF), lambda i: (i, 1)),
            pl.BlockSpec((3, DFF), lambda i: (0, 0)),
            pl.BlockSpec((DFF, D), lambda i: (0, 0), pipeline_mode=pl.Buffered(1)),
            pl.BlockSpec((tm, D), lambda i: (i, 0)),
            _mod_spec(5, tm),
        ],
        out_specs=pl.BlockSpec((tm, D), lambda i: (i, 0)),
        out_shape=jax.ShapeDtypeStruct((R, D), f32),
        scratch_shapes=[pltpu.VMEM((tm, DFF), bf16)],
        compiler_params=_cparams("arbitrary"),
        name="ffn_down",
    )(u, u, u, u, conv_ff_w, w_down, h, mods_l)


def _final_norm_kernel(h_ref, w_ref, o_ref):
    x = h_ref[...]
    o_ref[...] = x * lax.rsqrt(jnp.mean(x * x, axis=-1, keepdims=True) + EPS) * w_ref[...]


def _final_norm(h, w):
    tm = 512
    return pl.pallas_call(
        _final_norm_kernel,
        grid=(R_LAT // tm,),
        in_specs=[pl.BlockSpec((tm, D), lambda i: (i, 0)), pl.BlockSpec((1, D), lambda i: (0, 0))],
        out_specs=pl.BlockSpec((tm, D), lambda i: (i, 0)),
        out_shape=jax.ShapeDtypeStruct((R_LAT, D), f32),
        compiler_params=_cparams("arbitrary"),
        name="final_norm",
    )(h, w.reshape(1, D))


def kernel(x, c, ctx, c_ctx, w_mod, b_mod, norm1_w, norm2_w, w_in, b_in, conv_q_w, conv_k_w, mlstm_norm_w,
           w_pm, w_pf, w_pc, conv_c_w, w_o, w_up, conv_ff_w, w_down, final_norm_w):
    h = jnp.concatenate([x.reshape(R_LAT, D), ctx.reshape(R_CTX, D)], axis=0)
    cc8 = jnp.concatenate([c, c_ctx[None, :], jnp.zeros((8 - B - 1, D), f32)], axis=0)
    mods = _modulation(cc8, w_mod, b_mod).reshape(DEPTH, 8, 6, 1, D)

    w_main = jnp.concatenate([w_in[:, :, :REF_GATES], w_in[:, :, REF_Q:]], axis=2).astype(bf16)
    b_main = jnp.concatenate([b_in[:, :REF_GATES], b_in[:, REF_Q:]], axis=1)
    w_gate = jnp.pad(w_in[:, :, REF_GATES:REF_Q], ((0, 0), (0, 0), (0, GATE_PAD - 4 * NH))).astype(bf16)
    b_gate = jnp.pad(b_in[:, REF_GATES:REF_Q], ((0, 0), (0, GATE_PAD - 4 * NH)))

    for l in range(DEPTH):
        mods_l = mods[l]
        proj = _norm_matmul(h, norm1_w[l], mods_l, 0, 1, w_main[l], b_main[l], bf16, 1024, 1024, "in_proj")
        gates = _norm_matmul(h, norm1_w[l], mods_l, 0, 1, w_gate[l], b_gate[l], f32, 1024, GATE_PAD, "gate_proj")
        g4 = gates[:, :4 * NH].reshape(R, 4, NH)
        gcol = jnp.transpose(g4, (2, 0, 1))
        grow = jnp.transpose(g4, (2, 1, 0))
        k, q = _qk_conv(proj, conv_k_w[l], conv_q_w[l])
        hm = _mlstm(q, k, proj, gcol, grow, mlstm_norm_w[l])
        yf = _fourier(proj)
        h = _merge(hm, yf, proj, conv_c_w[l], w_pm[l].astype(bf16), w_pf[l].astype(bf16), w_pc[l].astype(bf16),
                   w_o[l].astype(bf16), h, mods_l)
        u = _norm_matmul(h, norm2_w[l], mods_l, 3, 4, w_up[l].astype(bf16), jnp.zeros((2 * DFF,), f32), bf16,
                         1024, 1024, "ffn_up")
        h = _ffn_down(u, conv_ff_w[l], w_down[l].astype(bf16), h, mods_l)

    return _final_norm(h, final_norm_w).reshape(B, T, D)
```

```python
import functools

import jax
import jax.numpy as jnp
from jax import lax
from jax.experimental import pallas as pl
from jax.experimental.pallas import tpu as pltpu

f32 = jnp.float32
bf16 = jnp.bfloat16

D = 2048
B = 4
T = 2048
T_CTX = 256
DEPTH = 4
GRID_W = 64
NH = 4
DH = 256
DM = NH * DH
NG = 4
DG = 128
DF = NG * DG
DC = 512
DFF = 5632
CHUNK = 128
EPS = 1e-6

R_LAT = B * T
R_CTX = B * T_CTX
R = R_LAT + R_CTX

COL_K, COL_V, COL_Q, COL_O = 0, DM, 2 * DM, 3 * DM
COL_F = 4 * DM
COL_CB = COL_F + DF
COL_CC = COL_CB + DC
COL_CX = COL_CC + DC
COL_G = COL_CX + DC
N_PROJ = COL_G + 3 * D
REF_GATES = 2 * DM
REF_Q = REF_GATES + 4 * NH
GATE_PAD = 128

VMEM_LIMIT = 56 * 1024 * 1024


def _cparams(*sem):
    return pltpu.CompilerParams(dimension_semantics=sem, vmem_limit_bytes=VMEM_LIMIT)


def _mod_row(row_start):
    return jnp.where(row_start < R_LAT, row_start // T, B)


def _mod_spec(l, comp, tm, row_blk0=0):
    return pl.BlockSpec((None, None, None, 1, D),
                        lambda i, *_: (l, _mod_row((row_blk0 + i) * tm), comp, 0, 0))


def _layer_spec(l, shape, col_map=None):
    if col_map is None:
        return pl.BlockSpec((None,) + shape, lambda *_: (l, 0, 0))
    return pl.BlockSpec((None,) + shape, lambda *idx: (l, 0, col_map(*idx)))


def _sigmoid(x):
    return jax.nn.sigmoid(x)


def _log_sigmoid(x):
    return jnp.minimum(x, 0.0) - jnp.log1p(jnp.exp(-jnp.abs(x)))


def _mod_kernel(c_ref, w_ref, b_ref, o_ref):
    c = c_ref[...]
    s = (c * _sigmoid(c)).astype(bf16)
    o_ref[...] = jnp.dot(s, w_ref[...].astype(bf16), preferred_element_type=f32) + b_ref[...]


def _modulation(cc8, w_mod, b_mod):
    tn = 1024
    n = 6 * D
    return pl.pallas_call(
        _mod_kernel,
        grid=(DEPTH, n // tn),
        in_specs=[
            pl.BlockSpec((8, D), lambda l, j: (0, 0)),
            pl.BlockSpec((None, D, tn), lambda l, j: (l, 0, j)),
            pl.BlockSpec((None, 1, tn), lambda l, j: (l, 0, j)),
        ],
        out_specs=pl.BlockSpec((None, 8, tn), lambda l, j: (l, 0, j)),
        out_shape=jax.ShapeDtypeStruct((DEPTH, 8, n), f32),
        compiler_params=_cparams("arbitrary", "arbitrary"),
        name="modulation",
    )(cc8, w_mod, b_mod.reshape(DEPTH, 1, n))


NORM_ROWS = 128
PROJ_TM = 1024
PROJ_TN = 1024


def _modulated_norm(h_ref, nw_ref, sh_ref, sc_ref, xn_ref):
    for r in range(0, h_ref.shape[0], NORM_ROWS):
        x = h_ref[r:r + NORM_ROWS, :]
        ms = jnp.mean(x * x, axis=-1, keepdims=True)
        y = x * lax.rsqrt(ms + EPS) * nw_ref[...]
        xn_ref[r:r + NORM_ROWS, :] = (y * (1.0 + sc_ref[...]) + sh_ref[...]).astype(bf16)


def _in_proj_kernel(h_ref, nw_ref, sh_ref, sc_ref, w_ref, b_ref, wg_ref, bg_ref, o_ref, g_ref, xn_ref):
    @pl.when(pl.program_id(1) == 0)
    def _():
        _modulated_norm(h_ref, nw_ref, sh_ref, sc_ref, xn_ref)
        g = jnp.dot(xn_ref[...], wg_ref[...], preferred_element_type=f32) + bg_ref[...]
        lane = lax.broadcasted_iota(jnp.int32, g.shape, 1)
        g_ref[...] = jnp.where((lane // NH) % 2 == 1, _log_sigmoid(g), g)

    acc = jnp.dot(xn_ref[...], w_ref[...], preferred_element_type=f32)
    o_ref[...] = (acc + b_ref[...]).astype(o_ref.dtype)


def _in_proj(l, h, norm1_w, mods, w_main, b_main, w_gate, b_gate):
    tm, tn = PROJ_TM, PROJ_TN
    return pl.pallas_call(
        _in_proj_kernel,
        grid=(R // tm, N_PROJ // tn),
        in_specs=[
            pl.BlockSpec((tm, D), lambda i, j: (i, 0)),
            _layer_spec(l, (1, D)),
            _mod_spec(l, 0, tm),
            _mod_spec(l, 1, tm),
            _layer_spec(l, (D, tn), lambda i, j: j),
            _layer_spec(l, (1, tn), lambda i, j: j),
            _layer_spec(l, (D, GATE_PAD)),
            _layer_spec(l, (1, GATE_PAD)),
        ],
        out_specs=[pl.BlockSpec((tm, tn), lambda i, j: (i, j)), pl.BlockSpec((tm, GATE_PAD), lambda i, j: (i, 0))],
        out_shape=[jax.ShapeDtypeStruct((R, N_PROJ), bf16), jax.ShapeDtypeStruct((R, GATE_PAD), f32)],
        scratch_shapes=[pltpu.VMEM((tm, D), bf16)],
        compiler_params=_cparams("arbitrary", "arbitrary"),
        name="in_proj",
    )(h, norm1_w, mods, mods, w_main, b_main, w_gate, b_gate)


def _ffn_up_kernel(h_ref, nw_ref, sh_ref, sc_ref, w_ref, o_ref, xn_ref):
    @pl.when(pl.program_id(1) == 0)
    def _():
        _modulated_norm(h_ref, nw_ref, sh_ref, sc_ref, xn_ref)

    o_ref[...] = jnp.dot(xn_ref[...], w_ref[...], preferred_element_type=f32).astype(o_ref.dtype)


def _ffn_up(l, h, norm2_w, mods, w_up, rows):
    tm, tn = PROJ_TM, PROJ_TN
    return pl.pallas_call(
        _ffn_up_kernel,
        grid=(rows // tm, 2 * DFF // tn),
        in_specs=[
            pl.BlockSpec((tm, D), lambda i, j: (i, 0)),
            _layer_spec(l, (1, D)),
            _mod_spec(l, 3, tm),
            _mod_spec(l, 4, tm),
            _layer_spec(l, (D, tn), lambda i, j: j),
        ],
        out_specs=pl.BlockSpec((tm, tn), lambda i, j: (i, j)),
        out_shape=jax.ShapeDtypeStruct((R, 2 * DFF), bf16),
        scratch_shapes=[pltpu.VMEM((tm, D), bf16)],
        compiler_params=_cparams("arbitrary", "arbitrary"),
        name="ffn_up",
    )(h, norm2_w, mods, mods, w_up)


QK_TR = 256
QK_TC = 512
HALO = 16


def _qk_conv_kernel(km, kp, kn, qm, qp, qn, wk_ref, wq_ref, ko_ref, qo_ref):
    r0 = pl.program_id(0) * QK_TR
    is_ctx = r0 >= R_LAT
    is_start = jnp.logical_or(is_ctx, r0 % T == 0)
    is_end = jnp.logical_or(is_ctx, (r0 + QK_TR) % T == 0)
    row = lax.broadcasted_iota(jnp.int32, (QK_TR, 1), 0)

    def conv_silu(m_ref, p_ref, n_ref, w_ref):
        x = m_ref[...].astype(f32)
        pv = jnp.where(is_start, 0.0, p_ref[...].astype(f32)[HALO - 1:HALO, :])
        nv = jnp.where(is_end, 0.0, n_ref[...].astype(f32)[0:1, :])
        x_prev = jnp.where(row == 0, pv, pltpu.roll(x, 1, 0))
        x_next = jnp.where(row == QK_TR - 1, nv, pltpu.roll(x, QK_TR - 1, 0))
        w = w_ref[...]
        a = x_prev * w[0:1, :] + x * w[1:2, :] + x_next * w[2:3, :]
        return a * _sigmoid(a)

    ko_ref[...] = conv_silu(km, kp, kn, wk_ref).astype(bf16)
    qo_ref[...] = (conv_silu(qm, qp, qn, wq_ref) * (DH ** -0.5)).astype(bf16)


def _qk_conv(l, proj, conv_k_w, conv_q_w):
    per = QK_TR // HALO
    nblk = R // HALO

    def main(col0):
        return pl.BlockSpec((QK_TR, QK_TC), lambda i, j: (i, col0 // QK_TC + j))

    def prev(col0):
        return pl.BlockSpec((HALO, QK_TC), lambda i, j: (jnp.maximum(i * per - 1, 0), col0 // QK_TC + j))

    def nxt(col0):
        return pl.BlockSpec((HALO, QK_TC), lambda i, j: (jnp.minimum((i + 1) * per, nblk - 1), col0 // QK_TC + j))

    wspec = _layer_spec(l, (3, QK_TC), lambda i, j: j)
    ospec = pl.BlockSpec((QK_TR, QK_TC), lambda i, j: (i, j))
    return pl.pallas_call(
        _qk_conv_kernel,
        grid=(R // QK_TR, DM // QK_TC),
        in_specs=[main(COL_K), prev(COL_K), nxt(COL_K), main(COL_Q), prev(COL_Q), nxt(COL_Q), wspec, wspec],
        out_specs=[ospec, ospec],
        out_shape=[jax.ShapeDtypeStruct((R, DM), bf16)] * 2,
        compiler_params=_cparams("arbitrary", "arbitrary"),
        name="qk_conv",
    )(proj, proj, proj, proj, proj, proj, conv_k_w, conv_q_w)


HP = 2
HPW = HP * DH


def _mlstm_chunks(refs, jobs, carries, c_ref):
    q_ref, k_ref, v_ref, gc_ref, gr_ref = refs
    t_idx = lax.broadcasted_iota(jnp.int32, (CHUNK, CHUNK), 0)
    s_idx = lax.broadcasted_iota(jnp.int32, (CHUNK, CHUNK), 1)
    nt = (((1,), (1,)), ((), ()))
    tn = (((0,), (0,)), ((), ()))

    ld = []
    for hh, c, _ in jobs:
        rows = pl.ds(pl.multiple_of(c * CHUNK, CHUNK), CHUNK)
        cols = slice(hh * DH, (hh + 1) * DH)
        ld.append((q_ref[rows, cols], k_ref[rows, cols], v_ref[rows, cols], rows))
    c_old = [c_ref[i] for i in range(len(jobs))]
    s_raw = [lax.dot_general(q, k, nt, preferred_element_type=f32) for q, k, _, _ in ld]
    q_c = [jnp.dot(q, co.astype(bf16), preferred_element_type=f32) for (q, _, _, _), co in zip(ld, c_old)]

    gate = []
    for (hh, c, reverse), (n, m), (_, _, _, rows) in zip(jobs, carries, ld):
        gcol = gc_ref[hh, rows, :]
        grow = gr_ref[hh, :, rows]
        gi, gf = (2, 3) if reverse else (0, 1)
        i_col = gcol[:, gi:gi + 1]
        lf_col = gcol[:, gf:gf + 1]
        i_row = grow[gi:gi + 1, :]
        lf_row = grow[gf:gf + 1, :]
        seen = (s_idx >= t_idx) if reverse else (s_idx <= t_idx)
        seen_t = (t_idx >= s_idx) if reverse else (t_idx <= s_idx)
        b_col = jnp.sum(jnp.where(seen, lf_row, 0.0), axis=1, keepdims=True)
        b_row = jnp.sum(jnp.where(seen_t, lf_col, 0.0), axis=0, keepdims=True)
        b_end = jnp.sum(lf_row, axis=1, keepdims=True)

        a_end = b_end - b_col + i_col
        m_new = jnp.maximum(b_end + m, jnp.max(a_end, axis=0, keepdims=True))
        w_end = jnp.exp(a_end - m_new)
        keep = jnp.exp(b_end + m - m_new)

        log_d = jnp.where(seen, b_col - b_row + i_row, -jnp.inf)
        inter = b_col + m
        m_t = jnp.maximum(inter, jnp.max(log_d, axis=1, keepdims=True))
        decay = jnp.exp(log_d - m_t)
        g_inter = jnp.exp(inter - m_t)
        gate.append((m_new, w_end, keep, m_t, decay, g_inter))

    s = [sr * g[4] for sr, g in zip(s_raw, gate)]
    s_v = [jnp.dot(si.astype(bf16), v, preferred_element_type=f32) for si, (_, _, v, _) in zip(s, ld)]
    kw = [k.astype(f32) * g[1] for (_, k, _, _), g in zip(ld, gate)]
    k_v = [lax.dot_general(kwi.astype(bf16), v, tn, preferred_element_type=f32) for kwi, (_, _, v, _) in zip(kw, ld)]

    hs, new = [], []
    for i, ((q, _, _, _), (n, m), (m_new, w_end, keep, m_t, decay, g_inter)) in enumerate(zip(ld, carries, gate)):
        num = g_inter * q_c[i] + s_v[i]
        qn = jnp.sum(q.astype(f32) * n, axis=1, keepdims=True)
        den = g_inter * qn + jnp.sum(s[i], axis=1, keepdims=True)
        hs.append(num / jnp.maximum(jnp.abs(den), jnp.exp(-m_t)))
        c_ref[i] = keep * c_old[i] + k_v[i]
        new.append((keep * n + jnp.sum(kw[i], axis=0, keepdims=True), m_new))
    return hs, tuple(new)


def _mlstm_kernel(ql, qc, kl, kc, vl, vc, gcl, gcc, grl, grc, ol, oc, nw_ref,
                  outl, outc, hbl, hbc, c_ref):
    lat = (ql, kl, vl, gcl, grl)
    ctx = (qc, kc, vc, gcc, grc)
    chains = [(hh, rev) for hh in range(HP) for rev in (False, True)]

    def emit(hb_ref, o_ref, out_ref, hh, c, h, final):
        rows = pl.ds(pl.multiple_of(c * CHUNK, CHUNK), CHUNK)
        cols = slice(hh * DH, (hh + 1) * DH)
        if not final:
            hb_ref[rows, cols] = h
            return
        ht = hb_ref[rows, cols] + h
        y = ht * lax.rsqrt(jnp.mean(ht * ht, axis=-1, keepdims=True) + EPS) * nw_ref[:, cols]
        out_ref[rows, cols] = (_sigmoid(o_ref[rows, cols].astype(f32)) * y).astype(bf16)

    def scan(refs, hb_ref, o_ref, out_ref, count, carries):
        def body(final, j, carries):
            jobs = [(hh, (count - 1 - j) if rev else j, rev) for hh, rev in chains]
            hs, carries = _mlstm_chunks(refs, jobs, carries, c_ref)
            for (hh, c, _), h in zip(jobs, hs):
                emit(hb_ref, o_ref, out_ref, hh, c, h, final)
            return carries

        carries = lax.fori_loop(0, count // 2, functools.partial(body, False), carries)
        return lax.fori_loop(count // 2, count, functools.partial(body, True), carries)

    c_ref[...] = jnp.zeros(c_ref.shape, f32)
    zero = tuple((jnp.zeros((1, DH), f32), jnp.zeros((1, 1), f32)) for _ in chains)
    carries = scan(ctx, hbc, oc, outc, T_CTX // CHUNK, zero)
    scan(lat, hbl, ol, outl, T // CHUNK, carries)


def _mlstm(l, q, k, proj, gcol, grow, norm_w):
    ctx_blk = R_LAT // T_CTX

    def lat(col0):
        return pl.BlockSpec((T, HPW), lambda b, h: (b, col0 // HPW + h))

    def ctx(col0):
        return pl.BlockSpec((T_CTX, HPW), lambda b, h: (ctx_blk + b, col0 // HPW + h))

    in_specs = [
        lat(0), ctx(0), lat(0), ctx(0), lat(COL_V), ctx(COL_V),
        pl.BlockSpec((HP, T, 4), lambda b, h: (h, b, 0)),
        pl.BlockSpec((HP, T_CTX, 4), lambda b, h: (h, ctx_blk + b, 0)),
        pl.BlockSpec((HP, 4, T), lambda b, h: (h, 0, b)),
        pl.BlockSpec((HP, 4, T_CTX), lambda b, h: (h, 0, ctx_blk + b)),
        lat(COL_O), ctx(COL_O),
        _layer_spec(l, (1, HPW), lambda b, h: h),
    ]
    return pl.pallas_call(
        _mlstm_kernel,
        grid=(B, NH // HP),
        in_specs=in_specs,
        out_specs=[pl.BlockSpec((T, HPW), lambda b, h: (b, h)), pl.BlockSpec((T_CTX, HPW), lambda b, h: (b, h))],
        out_shape=[jax.ShapeDtypeStruct((R_LAT, DM), bf16), jax.ShapeDtypeStruct((R_CTX, DM), bf16)],
        scratch_shapes=[pltpu.VMEM((T, HPW), f32), pltpu.VMEM((T_CTX, HPW), f32),
                        pltpu.VMEM((2 * HP, DH, DH), f32)],
        compiler_params=_cparams("arbitrary", "arbitrary"),
        name="mlstm",
    )(q, q, k, k, proj, proj, gcol, gcol, grow, grow, proj, proj, norm_w)


def _dft_tables(t_len):
    def cs(n):
        i = lax.broadcasted_iota(jnp.int32, (n, n), 0)
        j = lax.broadcasted_iota(jnp.int32, (n, n), 1)
        ang = ((i * j) % n).astype(f32) * (2.0 * jnp.pi / n)
        return jnp.cos(ang), jnp.sin(ang)

    ct, st = cs(t_len)
    cg, sg = cs(DG)
    scale = (t_len * DG) ** -0.5
    return (jnp.concatenate([ct, st], axis=1).astype(bf16),
            (jnp.concatenate([cg, -sg], axis=1) * scale).astype(bf16))


def _fourier_kernel(x_ref, cs_ref, f_ref, o_ref, z_ref, *, t_len):
    x = x_ref[...]
    for g in range(NG):
        cols = slice(g * DG, (g + 1) * DG)
        pq = jnp.dot(x[:, cols], cs_ref[...], preferred_element_type=f32)
        z_ref[0:t_len, cols] = pq[:, :DG].astype(bf16)
        z_ref[t_len:2 * t_len, cols] = pq[:, DG:].astype(bf16)
    o_ref[...] = jnp.dot(f_ref[...], z_ref[...], preferred_element_type=f32).astype(bf16)


def _fourier_call(proj, t_len, row_blk0, prev_out):
    f_tab, g_tab = _dft_tables(t_len)
    kern = functools.partial(_fourier_kernel, t_len=t_len)
    in_specs = [
        pl.BlockSpec((t_len, DF), lambda b: (row_blk0 + b, COL_F // DF)),
        pl.BlockSpec((DG, 2 * DG), lambda b: (0, 0)),
        pl.BlockSpec((t_len, 2 * t_len), lambda b: (0, 0), pipeline_mode=pl.Buffered(1)),
    ]
    args = [proj, g_tab, f_tab]
    aliases = {}
    if prev_out is not None:
        in_specs.append(pl.BlockSpec(memory_space=pl.ANY))
        args.append(prev_out)
        aliases = {3: 0}

        def body(x_ref, cs_ref, f_ref, _, o_ref, z_ref):
            kern(x_ref, cs_ref, f_ref, o_ref, z_ref)
    else:
        body = kern
    return pl.pallas_call(
        body,
        grid=(B,),
        in_specs=in_specs,
        out_specs=pl.BlockSpec((t_len, DF), lambda b: (row_blk0 + b, 0)),
        out_shape=jax.ShapeDtypeStruct((R, DF), bf16),
        scratch_shapes=[pltpu.VMEM((2 * t_len, DF), bf16)],
        input_output_aliases=aliases,
        compiler_params=_cparams("arbitrary"),
        name=f"fourier_{t_len}",
    )(*args)


def _fourier(proj, with_ctx):
    y = _fourier_call(proj, T, 0, None)
    return _fourier_call(proj, T_CTX, R_LAT // T_CTX, y) if with_ctx else y


MG_TM = 256


def _merge_kernel(hml_ref, hmc_ref, yf_ref, cb_ref, cc_ref, cx_ref, wc_ref, gm_ref, wpm_ref, wpf_ref, wpc_ref,
                  wo_ref, h_ref, g1_ref, o_ref):
    r0 = pl.program_id(0) * MG_TM
    is_ctx = r0 >= R_LAT
    period = jnp.where(is_ctx, T_CTX, GRID_W)
    row = lax.broadcasted_iota(jnp.int32, (MG_TM, 1), 0)
    pos = row % period
    ccx = cc_ref[...].astype(f32) * cx_ref[...].astype(f32)
    x_prev = jnp.where(pos == 0, 0.0, pltpu.roll(ccx, 1, 0))
    x_next = jnp.where(pos == period - 1, 0.0, pltpu.roll(ccx, MG_TM - 1, 0))
    wc = wc_ref[...]
    uc = cb_ref[...].astype(f32) * (x_prev * wc[0:1, :] + ccx * wc[1:2, :] + x_next * wc[2:3, :])

    hm = jnp.where(is_ctx, hmc_ref[...], hml_ref[...])
    y_m = jnp.dot(hm, wpm_ref[...], preferred_element_type=f32)
    y_f = jnp.dot(yf_ref[...], wpf_ref[...], preferred_element_type=f32)
    y_c = jnp.dot(uc.astype(bf16), wpc_ref[...], preferred_element_type=f32)
    merged = _sigmoid(gm_ref[:, 0:D].astype(f32)) * y_m
    merged = merged + _sigmoid(gm_ref[:, D:2 * D].astype(f32)) * y_f
    merged = merged + _sigmoid(gm_ref[:, 2 * D:3 * D].astype(f32)) * y_c
    out = jnp.dot(merged.astype(bf16), wo_ref[...], preferred_element_type=f32)
    o_ref[...] = h_ref[...] + g1_ref[...] * out


def _merge(l, hm_lat, hm_ctx, yf, proj, conv_c_w, w_pm, w_pf, w_pc, w_o, h, mods, rows):
    tm = MG_TM
    n_lat, n_ctx = R_LAT // tm, R_CTX // tm

    def const(shape):
        return pl.BlockSpec((None,) + shape, lambda i: (l, 0, 0), pipeline_mode=pl.Buffered(1))

    return pl.pallas_call(
        _merge_kernel,
        grid=(rows // tm,),
        in_specs=[
            pl.BlockSpec((tm, DM), lambda i: (jnp.minimum(i, n_lat - 1), 0)),
            pl.BlockSpec((tm, DM), lambda i: (jnp.clip(i - n_lat, 0, n_ctx - 1), 0)),
            pl.BlockSpec((tm, DF), lambda i: (i, 0)),
            pl.BlockSpec((tm, DC), lambda i: (i, COL_CB // DC)),
            pl.BlockSpec((tm, DC), lambda i: (i, COL_CC // DC)),
            pl.BlockSpec((tm, DC), lambda i: (i, COL_CX // DC)),
            _layer_spec(l, (3, DC)),
            pl.BlockSpec((tm, 3 * D), lambda i: (i, COL_G // (3 * D))),
            const((DM, D)), const((DF, D)), const((DC, D)), const((D, D)),
            pl.BlockSpec((tm, D), lambda i: (i, 0)),
            _mod_spec(l, 2, tm),
        ],
        out_specs=pl.BlockSpec((tm, D), lambda i: (i, 0)),
        out_shape=jax.ShapeDtypeStruct((R, D), f32),
        compiler_params=_cparams("arbitrary"),
        name="merge",
    )(hm_lat, hm_ctx, yf, proj, proj, proj, conv_c_w, proj, w_pm, w_pf, w_pc, w_o, h, mods)


FF_TM = 256
FF_TC = 1408


def _ffn_down_kernel(*refs, is_ctx):
    if is_ctx:
        ua_ref, ub_ref, wf_ref, wd_ref, h_ref, g2_ref, _, o_ref = refs
    else:
        ua_ref, up_ref, un_ref, ub_ref, wf_ref, wd_ref, h_ref, g2_ref, o_ref = refs
    r0 = pl.program_id(0) * FF_TM
    row = lax.broadcasted_iota(jnp.int32, (FF_TM, 1), 0)
    acc = None
    for c0 in range(0, DFF, FF_TC):
        cols = slice(c0, c0 + FF_TC)
        a = ua_ref[:, cols].astype(f32)
        if is_ctx:
            a_prev = jnp.where(row == 0, 0.0, pltpu.roll(a, 1, 0))
            a_next = jnp.where(row == FF_TM - 1, 0.0, pltpu.roll(a, FF_TM - 1, 0))
        else:
            top = jnp.where(r0 % T == 0, 0.0, up_ref[:, cols].astype(f32))
            bot = jnp.where((r0 + FF_TM) % T == 0, 0.0, un_ref[:, cols].astype(f32))
            a_prev = jnp.concatenate([top, a[:FF_TM - GRID_W, :]], axis=0)
            a_next = jnp.concatenate([a[GRID_W:, :], bot], axis=0)
        wf = wf_ref[:, cols]
        x = a_prev * wf[0:1, :] + a * wf[1:2, :] + a_next * wf[2:3, :]
        act = (x * _sigmoid(x) * ub_ref[:, cols].astype(f32)).astype(bf16)
        part = jnp.dot(act, wd_ref[cols, :], preferred_element_type=f32)
        acc = part if acc is None else acc + part
    o_ref[...] = h_ref[...] + g2_ref[...] * acc


def _ffn_down(l, u, conv_ff_w, w_down, h, mods, prev_out):
    assert T_CTX == FF_TM
    tm = FF_TM
    is_ctx = prev_out is not None
    blk0 = R_LAT // tm if is_ctx else 0
    per = tm // GRID_W
    row_spec = lambda cb: pl.BlockSpec((tm, DFF), lambda i: (blk0 + i, cb))
    in_specs = [row_spec(0)]
    args = [u]
    if not is_ctx:
        in_specs += [
            pl.BlockSpec((GRID_W, DFF), lambda i: (jnp.maximum(i * per - 1, 0), 0)),
            pl.BlockSpec((GRID_W, DFF), lambda i: (jnp.minimum((i + 1) * per, R_LAT // GRID_W - 1), 0)),
        ]
        args += [u, u]
    in_specs += [
        row_spec(1),
        _layer_spec(l, (3, DFF)),
        pl.BlockSpec((None, DFF, D), lambda i: (l, 0, 0), pipeline_mode=pl.Buffered(1)),
        pl.BlockSpec((tm, D), lambda i: (blk0 + i, 0)),
        _mod_spec(l, 5, tm, blk0),
    ]
    args += [u, conv_ff_w, w_down, h, mods]
    aliases = {}
    if is_ctx:
        in_specs.append(pl.BlockSpec(memory_space=pl.ANY))
        args.append(prev_out)
        aliases = {len(args) - 1: 0}
    return pl.pallas_call(
        functools.partial(_ffn_down_kernel, is_ctx=is_ctx),
        grid=((R_CTX if is_ctx else R_LAT) // tm,),
        in_specs=in_specs,
        out_specs=pl.BlockSpec((tm, D), lambda i: (blk0 + i, 0)),
        out_shape=jax.ShapeDtypeStruct((R, D), f32),
        input_output_aliases=aliases,
        compiler_params=_cparams("arbitrary"),
        name="ffn_down_ctx" if is_ctx else "ffn_down",
    )(*args)


def _final_norm_kernel(h_ref, w_ref, o_ref):
    x = h_ref[...]
    o_ref[...] = x * lax.rsqrt(jnp.mean(x * x, axis=-1, keepdims=True) + EPS) * w_ref[...]


def _final_norm(h, w):
    tm = 512
    return pl.pallas_call(
        _final_norm_kernel,
        grid=(R_LAT // tm,),
        in_specs=[pl.BlockSpec((tm, D), lambda i: (i, 0)), pl.BlockSpec((1, D), lambda i: (0, 0))],
        out_specs=pl.BlockSpec((tm, D), lambda i: (i, 0)),
        out_shape=jax.ShapeDtypeStruct((R_LAT, D), f32),
        compiler_params=_cparams("arbitrary"),
        name="final_norm",
    )(h, w.reshape(1, D))


def kernel(x, c, ctx, c_ctx, w_mod, b_mod, norm1_w, norm2_w, w_in, b_in, conv_q_w, conv_k_w, mlstm_norm_w,
           w_pm, w_pf, w_pc, conv_c_w, w_o, w_up, conv_ff_w, w_down, final_norm_w):
    h = jnp.concatenate([x.reshape(R_LAT, D), ctx.reshape(R_CTX, D)], axis=0)
    cc8 = jnp.concatenate([c, c_ctx[None, :], jnp.zeros((8 - B - 1, D), f32)], axis=0)
    mods = _modulation(cc8, w_mod, b_mod).reshape(DEPTH, 8, 6, 1, D)

    w_main = jnp.concatenate([w_in[:, :, :REF_GATES], w_in[:, :, REF_Q:]], axis=2).astype(bf16)
    b_main = jnp.concatenate([b_in[:, :REF_GATES], b_in[:, REF_Q:]], axis=1).reshape(DEPTH, 1, N_PROJ)
    w_gate = jnp.pad(w_in[:, :, REF_GATES:REF_Q], ((0, 0), (0, 0), (0, GATE_PAD - 4 * NH))).astype(bf16)
    b_gate = jnp.pad(b_in[:, REF_GATES:REF_Q], ((0, 0), (0, GATE_PAD - 4 * NH))).reshape(DEPTH, 1, GATE_PAD)
    w_pm, w_pf, w_pc, w_o, w_up, w_down = (w.astype(bf16) for w in (w_pm, w_pf, w_pc, w_o, w_up, w_down))
    norm1_w = norm1_w.reshape(DEPTH, 1, D)
    norm2_w = norm2_w.reshape(DEPTH, 1, D)
    mlstm_norm_w = mlstm_norm_w.reshape(DEPTH, 1, DM)

    for l in range(DEPTH):
        last = l == DEPTH - 1
        rows = R_LAT if last else R
        proj, gates = _in_proj(l, h, norm1_w, mods, w_main, b_main, w_gate, b_gate)
        g4 = gates[:, :4 * NH].reshape(R, 4, NH)
        gcol = jnp.transpose(g4, (2, 0, 1))
        grow = jnp.transpose(g4, (2, 1, 0))
        k, q = _qk_conv(l, proj, conv_k_w, conv_q_w)
        hm_lat, hm_ctx = _mlstm(l, q, k, proj, gcol, grow, mlstm_norm_w)
        yf = _fourier(proj, not last)
        h = _merge(l, hm_lat, hm_ctx, yf, proj, conv_c_w, w_pm, w_pf, w_pc, w_o, h, mods, rows)
        u = _ffn_up(l, h, norm2_w, mods, w_up, rows)
        h_new = _ffn_down(l, u, conv_ff_w, w_down, h, mods, None)
        h = h_new if last else _ffn_down(l, u, conv_ff_w, w_down, h, mods, h_new)

    return _final_norm(h, final_norm_w).reshape(B, T, D)
```

```python
import functools

import jax
import numpy as np
import jax.numpy as jnp
from jax import lax
from jax.experimental import pallas as pl
from jax.experimental.pallas import tpu as pltpu

f32 = jnp.float32
bf16 = jnp.bfloat16

D = 2048
B = 4
T = 2048
T_CTX = 256
DEPTH = 4
GRID_W = 64
NH = 4
DH = 256
DM = NH * DH
NG = 4
DG = 128
DF = NG * DG
DC = 512
DFF = 5632
CHUNK = 128
EPS = 1e-6

R_LAT = B * T
R_CTX = B * T_CTX
R = R_LAT + R_CTX

COL_K, COL_V, COL_Q, COL_O = 0, DM, 2 * DM, 3 * DM
COL_F = 4 * DM
COL_CB = COL_F + DF
COL_CC = COL_CB + DC
COL_CX = COL_CC + DC
COL_G = COL_CX + DC
N_PROJ = COL_G + 3 * D
REF_GATES = 2 * DM
REF_Q = REF_GATES + 4 * NH
GATE_PAD = 128

VMEM_LIMIT = 56 * 1024 * 1024


def _cparams(*sem):
    return pltpu.CompilerParams(dimension_semantics=sem, vmem_limit_bytes=VMEM_LIMIT)


def _mod_row(row_start):
    return jnp.where(row_start < R_LAT, row_start // T, B)


def _mod_spec(l, comp, tm, row_blk0=0):
    return pl.BlockSpec((None, None, None, 1, D),
                        lambda i, *_: (l, _mod_row((row_blk0 + i) * tm), comp, 0, 0))


def _layer_spec(l, shape, col_map=None):
    if col_map is None:
        return pl.BlockSpec((None,) + shape, lambda *_: (l, 0, 0))
    return pl.BlockSpec((None,) + shape, lambda *idx: (l, 0, col_map(*idx)))


def _sigmoid(x):
    return jax.nn.sigmoid(x)


def _log_sigmoid(x):
    return jnp.minimum(x, 0.0) - jnp.log1p(jnp.exp(-jnp.abs(x)))


def _mod_kernel(c_ref, w_ref, b_ref, o_ref):
    c = c_ref[...]
    s = (c * _sigmoid(c)).astype(bf16)
    o_ref[...] = jnp.dot(s, w_ref[...].astype(bf16), preferred_element_type=f32) + b_ref[...]


def _modulation(cc8, w_mod, b_mod):
    tn = 1024
    n = 6 * D
    return pl.pallas_call(
        _mod_kernel,
        grid=(DEPTH, n // tn),
        in_specs=[
            pl.BlockSpec((8, D), lambda l, j: (0, 0)),
            pl.BlockSpec((None, D, tn), lambda l, j: (l, 0, j)),
            pl.BlockSpec((None, 1, tn), lambda l, j: (l, 0, j)),
        ],
        out_specs=pl.BlockSpec((None, 8, tn), lambda l, j: (l, 0, j)),
        out_shape=jax.ShapeDtypeStruct((DEPTH, 8, n), f32),
        compiler_params=_cparams("arbitrary", "arbitrary"),
        name="modulation",
    )(cc8, w_mod, b_mod.reshape(DEPTH, 1, n))


PREP_TN = 1024
PREP_ROWS = 256
LANES = 128
GATE_COLS = REF_Q - REF_GATES


def _w_in_prep_kernel(a_ref, b_ref, o_ref):
    j = pl.program_id(1)

    @pl.when(j < REF_GATES // PREP_TN)
    def _():
        o_ref[...] = a_ref[...].astype(bf16)

    @pl.when(j >= REF_GATES // PREP_TN)
    def _():
        for r in range(0, D, PREP_ROWS):
            cat = jnp.concatenate([a_ref[r:r + PREP_ROWS, :], b_ref[r:r + PREP_ROWS, :]], axis=1)
            o_ref[r:r + PREP_ROWS, :] = cat[:, GATE_COLS:GATE_COLS + PREP_TN].astype(bf16)


def _w_in_prep(w_in):
    return pl.pallas_call(
        _w_in_prep_kernel,
        grid=(DEPTH, N_PROJ // PREP_TN),
        in_specs=[
            pl.BlockSpec((None, D, PREP_TN), lambda l, j: (l, 0, j)),
            pl.BlockSpec((None, D, LANES), lambda l, j: (l, 0, (j + 1) * (PREP_TN // LANES))),
        ],
        out_specs=pl.BlockSpec((None, D, PREP_TN), lambda l, j: (l, 0, j)),
        out_shape=jax.ShapeDtypeStruct((DEPTH, D, N_PROJ), bf16),
        compiler_params=_cparams("arbitrary", "arbitrary"),
        name="w_in_prep",
    )(w_in, w_in)


NORM_ROWS = 128
PROJ_TM = 1024
PROJ_TN = 1024
IN_TN = 2048


def _modulated_norm(h_ref, nw_ref, sh_ref, sc_ref, xn_ref):
    for r in range(0, h_ref.shape[0], NORM_ROWS):
        x = h_ref[r:r + NORM_ROWS, :]
        ms = jnp.mean(x * x, axis=-1, keepdims=True)
        y = x * lax.rsqrt(ms + EPS) * nw_ref[...]
        xn_ref[r:r + NORM_ROWS, :] = (y * (1.0 + sc_ref[...]) + sh_ref[...]).astype(bf16)


def _in_proj_kernel(h_ref, nw_ref, sh_ref, sc_ref, w_ref, b_ref, wg_ref, bg_ref, o_ref, g_ref, xn_ref):
    @pl.when(pl.program_id(1) == 0)
    def _():
        _modulated_norm(h_ref, nw_ref, sh_ref, sc_ref, xn_ref)
        g = jnp.dot(xn_ref[...], wg_ref[...], preferred_element_type=f32) + bg_ref[...]
        lane = lax.broadcasted_iota(jnp.int32, g.shape, 1)
        g_ref[...] = jnp.where((lane // NH) % 2 == 1, _log_sigmoid(g), g)

    acc = jnp.dot(xn_ref[...], w_ref[...], preferred_element_type=f32)
    o_ref[...] = (acc + b_ref[...]).astype(o_ref.dtype)


def _in_proj(l, h, norm1_w, mods, w_main, b_main, w_gate, b_gate):
    tm, tn = PROJ_TM, IN_TN
    return pl.pallas_call(
        _in_proj_kernel,
        grid=(R // tm, N_PROJ // tn),
        in_specs=[
            pl.BlockSpec((tm, D), lambda i, j: (i, 0)),
            _layer_spec(l, (1, D)),
            _mod_spec(l, 0, tm),
            _mod_spec(l, 1, tm),
            _layer_spec(l, (D, tn), lambda i, j: j),
            _layer_spec(l, (1, tn), lambda i, j: j),
            _layer_spec(l, (D, GATE_PAD)),
            _layer_spec(l, (1, GATE_PAD)),
        ],
        out_specs=[pl.BlockSpec((tm, tn), lambda i, j: (i, j)), pl.BlockSpec((tm, GATE_PAD), lambda i, j: (i, 0))],
        out_shape=[jax.ShapeDtypeStruct((R, N_PROJ), bf16), jax.ShapeDtypeStruct((R, GATE_PAD), f32)],
        scratch_shapes=[pltpu.VMEM((tm, D), bf16)],
        compiler_params=_cparams("arbitrary", "arbitrary"),
        name="in_proj",
    )(h, norm1_w, mods, mods, w_main, b_main, w_gate, b_gate)


def _ffn_up_kernel(h_ref, nw_ref, sh_ref, sc_ref, w_ref, o_ref, xn_ref):
    @pl.when(pl.program_id(1) == 0)
    def _():
        _modulated_norm(h_ref, nw_ref, sh_ref, sc_ref, xn_ref)

    o_ref[...] = jnp.dot(xn_ref[...], w_ref[...].astype(bf16), preferred_element_type=f32).astype(o_ref.dtype)


def _ffn_up(l, h, norm2_w, mods, w_up, rows):
    tm, tn = PROJ_TM, PROJ_TN
    return pl.pallas_call(
        _ffn_up_kernel,
        grid=(rows // tm, 2 * DFF // tn),
        in_specs=[
            pl.BlockSpec((tm, D), lambda i, j: (i, 0)),
            _layer_spec(l, (1, D)),
            _mod_spec(l, 3, tm),
            _mod_spec(l, 4, tm),
            _layer_spec(l, (D, tn), lambda i, j: j),
        ],
        out_specs=pl.BlockSpec((tm, tn), lambda i, j: (i, j)),
        out_shape=jax.ShapeDtypeStruct((R, 2 * DFF), bf16),
        scratch_shapes=[pltpu.VMEM((tm, D), bf16)],
        compiler_params=_cparams("arbitrary", "arbitrary"),
        name="ffn_up",
    )(h, norm2_w, mods, mods, w_up)


QK_TR = 256
QK_TC = DM
HALO = 16


def _qk_conv_kernel(km, kp, kn, qm, qp, qn, wk_ref, wq_ref, ko_ref, qo_ref):
    r0 = pl.program_id(0) * QK_TR
    is_ctx = r0 >= R_LAT
    is_start = jnp.logical_or(is_ctx, r0 % T == 0)
    is_end = jnp.logical_or(is_ctx, (r0 + QK_TR) % T == 0)
    row = lax.broadcasted_iota(jnp.int32, (QK_TR, 1), 0)

    def conv_silu(m_ref, p_ref, n_ref, w_ref):
        x = m_ref[...].astype(f32)
        pv = jnp.where(is_start, 0.0, p_ref[...].astype(f32)[HALO - 1:HALO, :])
        nv = jnp.where(is_end, 0.0, n_ref[...].astype(f32)[0:1, :])
        x_prev = jnp.where(row == 0, pv, pltpu.roll(x, 1, 0))
        x_next = jnp.where(row == QK_TR - 1, nv, pltpu.roll(x, QK_TR - 1, 0))
        w = w_ref[...]
        a = x_prev * w[0:1, :] + x * w[1:2, :] + x_next * w[2:3, :]
        return a * _sigmoid(a)

    ko_ref[...] = conv_silu(km, kp, kn, wk_ref).astype(bf16)
    qo_ref[...] = (conv_silu(qm, qp, qn, wq_ref) * (DH ** -0.5)).astype(bf16)


def _qk_conv(l, proj, conv_k_w, conv_q_w):
    per = QK_TR // HALO
    nblk = R // HALO

    def main(col0):
        return pl.BlockSpec((QK_TR, QK_TC), lambda i, j: (i, col0 // QK_TC + j))

    def prev(col0):
        return pl.BlockSpec((HALO, QK_TC), lambda i, j: (jnp.maximum(i * per - 1, 0), col0 // QK_TC + j))

    def nxt(col0):
        return pl.BlockSpec((HALO, QK_TC), lambda i, j: (jnp.minimum((i + 1) * per, nblk - 1), col0 // QK_TC + j))

    wspec = _layer_spec(l, (3, QK_TC), lambda i, j: j)
    ospec = pl.BlockSpec((QK_TR, QK_TC), lambda i, j: (i, j))
    return pl.pallas_call(
        _qk_conv_kernel,
        grid=(R // QK_TR, DM // QK_TC),
        in_specs=[main(COL_K), prev(COL_K), nxt(COL_K), main(COL_Q), prev(COL_Q), nxt(COL_Q), wspec, wspec],
        out_specs=[ospec, ospec],
        out_shape=[jax.ShapeDtypeStruct((R, DM), bf16)] * 2,
        compiler_params=_cparams("arbitrary", "arbitrary"),
        name="qk_conv",
    )(proj, proj, proj, proj, proj, proj, conv_k_w, conv_q_w)


HP = 2
HPW = HP * DH


def _mlstm_chunks(refs, jobs, carries, c_ref):
    q_ref, k_ref, v_ref, gc_ref, gr_ref = refs
    t_idx = lax.broadcasted_iota(jnp.int32, (CHUNK, CHUNK), 0)
    s_idx = lax.broadcasted_iota(jnp.int32, (CHUNK, CHUNK), 1)
    nt = (((1,), (1,)), ((), ()))
    tn = (((0,), (0,)), ((), ()))

    ld = []
    for hh, c, _ in jobs:
        rows = pl.ds(pl.multiple_of(c * CHUNK, CHUNK), CHUNK)
        cols = slice(hh * DH, (hh + 1) * DH)
        ld.append((q_ref[rows, cols], k_ref[rows, cols], v_ref[rows, cols], rows))
    c_old = [c_ref[i] for i in range(len(jobs))]
    s_raw = [lax.dot_general(q, k, nt, preferred_element_type=f32) for q, k, _, _ in ld]
    q_c = [jnp.dot(q, co.astype(bf16), preferred_element_type=f32) for (q, _, _, _), co in zip(ld, c_old)]

    gate = []
    for (hh, c, reverse), (n, m), (_, _, _, rows) in zip(jobs, carries, ld):
        gcol = gc_ref[hh, rows, :]
        grow = gr_ref[hh, :, rows]
        gi, gf = (2, 3) if reverse else (0, 1)
        i_col = gcol[:, gi:gi + 1]
        lf_col = gcol[:, gf:gf + 1]
        i_row = grow[gi:gi + 1, :]
        lf_row = grow[gf:gf + 1, :]
        seen = (s_idx >= t_idx) if reverse else (s_idx <= t_idx)
        seen_t = (t_idx >= s_idx) if reverse else (t_idx <= s_idx)
        b_col = jnp.sum(jnp.where(seen, lf_row, 0.0), axis=1, keepdims=True)
        b_row = jnp.sum(jnp.where(seen_t, lf_col, 0.0), axis=0, keepdims=True)
        b_end = jnp.sum(lf_row, axis=1, keepdims=True)

        a_end = b_end - b_col + i_col
        m_new = jnp.maximum(b_end + m, jnp.max(a_end, axis=0, keepdims=True))
        w_end = jnp.exp(a_end - m_new)
        keep = jnp.exp(b_end + m - m_new)

        log_d = jnp.where(seen, b_col - b_row + i_row, -jnp.inf)
        inter = b_col + m
        m_t = jnp.maximum(inter, jnp.max(log_d, axis=1, keepdims=True))
        decay = jnp.exp(log_d - m_t)
        g_inter = jnp.exp(inter - m_t)
        gate.append((m_new, w_end, keep, m_t, decay, g_inter))

    s = [sr * g[4] for sr, g in zip(s_raw, gate)]
    s_v = [jnp.dot(si.astype(bf16), v, preferred_element_type=f32) for si, (_, _, v, _) in zip(s, ld)]
    kw = [k.astype(f32) * g[1] for (_, k, _, _), g in zip(ld, gate)]
    k_v = [lax.dot_general(kwi.astype(bf16), v, tn, preferred_element_type=f32) for kwi, (_, _, v, _) in zip(kw, ld)]

    hs, new = [], []
    for i, ((q, _, _, _), (n, m), (m_new, w_end, keep, m_t, decay, g_inter)) in enumerate(zip(ld, carries, gate)):
        num = g_inter * q_c[i] + s_v[i]
        qn = jnp.sum(q.astype(f32) * n, axis=1, keepdims=True)
        den = g_inter * qn + jnp.sum(s[i], axis=1, keepdims=True)
        hs.append(num / jnp.maximum(jnp.abs(den), jnp.exp(-m_t)))
        c_ref[i] = keep * c_old[i] + k_v[i]
        new.append((keep * n + jnp.sum(kw[i], axis=0, keepdims=True), m_new))
    return hs, tuple(new)


def _mlstm_kernel(ql, qc, kl, kc, vl, vc, gcl, gcc, grl, grc, ol, oc, nw_ref,
                  outl, outc, hbl, hbc, c_ref):
    lat = (ql, kl, vl, gcl, grl)
    ctx = (qc, kc, vc, gcc, grc)
    chains = [(hh, rev) for hh in range(HP) for rev in (False, True)]

    def emit(hb_ref, o_ref, out_ref, hh, c, h, final):
        rows = pl.ds(pl.multiple_of(c * CHUNK, CHUNK), CHUNK)
        cols = slice(hh * DH, (hh + 1) * DH)
        if not final:
            hb_ref[rows, cols] = h
            return
        ht = hb_ref[rows, cols] + h
        y = ht * lax.rsqrt(jnp.mean(ht * ht, axis=-1, keepdims=True) + EPS) * nw_ref[:, cols]
        out_ref[rows, cols] = (_sigmoid(o_ref[rows, cols].astype(f32)) * y).astype(bf16)

    def scan(refs, hb_ref, o_ref, out_ref, count, carries):
        def body(final, j, carries):
            jobs = [(hh, (count - 1 - j) if rev else j, rev) for hh, rev in chains]
            hs, carries = _mlstm_chunks(refs, jobs, carries, c_ref)
            for (hh, c, _), h in zip(jobs, hs):
                emit(hb_ref, o_ref, out_ref, hh, c, h, final)
            return carries

        carries = lax.fori_loop(0, count // 2, functools.partial(body, False), carries)
        return lax.fori_loop(count // 2, count, functools.partial(body, True), carries)

    c_ref[...] = jnp.zeros(c_ref.shape, f32)
    zero = tuple((jnp.zeros((1, DH), f32), jnp.zeros((1, 1), f32)) for _ in chains)
    carries = scan(ctx, hbc, oc, outc, T_CTX // CHUNK, zero)
    scan(lat, hbl, ol, outl, T // CHUNK, carries)


def _mlstm(l, q, k, proj, gcol, grow, norm_w):
    ctx_blk = R_LAT // T_CTX

    def lat(col0):
        return pl.BlockSpec((T, HPW), lambda b, h: (b, col0 // HPW + h))

    def ctx(col0):
        return pl.BlockSpec((T_CTX, HPW), lambda b, h: (ctx_blk + b, col0 // HPW + h))

    in_specs = [
        lat(0), ctx(0), lat(0), ctx(0), lat(COL_V), ctx(COL_V),
        pl.BlockSpec((HP, T, 4), lambda b, h: (h, b, 0)),
        pl.BlockSpec((HP, T_CTX, 4), lambda b, h: (h, ctx_blk + b, 0)),
        pl.BlockSpec((HP, 4, T), lambda b, h: (h, 0, b)),
        pl.BlockSpec((HP, 4, T_CTX), lambda b, h: (h, 0, ctx_blk + b)),
        lat(COL_O), ctx(COL_O),
        _layer_spec(l, (1, HPW), lambda b, h: h),
    ]
    return pl.pallas_call(
        _mlstm_kernel,
        grid=(B, NH // HP),
        in_specs=in_specs,
        out_specs=[pl.BlockSpec((T, HPW), lambda b, h: (b, h)), pl.BlockSpec((T_CTX, HPW), lambda b, h: (b, h))],
        out_shape=[jax.ShapeDtypeStruct((R_LAT, DM), bf16), jax.ShapeDtypeStruct((R_CTX, DM), bf16)],
        scratch_shapes=[pltpu.VMEM((T, HPW), f32), pltpu.VMEM((T_CTX, HPW), f32),
                        pltpu.VMEM((2 * HP, DH, DH), f32)],
        compiler_params=_cparams("arbitrary", "arbitrary"),
        name="mlstm",
    )(q, q, k, k, proj, proj, gcol, gcol, grow, grow, proj, proj, norm_w)


def _dft_tables(t_len):
    def cs(n):
        i = np.arange(n, dtype=np.int64)
        ang = ((i[:, None] * i[None, :]) % n) * (2.0 * np.pi / n)
        return np.cos(ang), np.sin(ang)

    ct, st = cs(t_len)
    cg, sg = cs(DG)
    scale = (t_len * DG) ** -0.5
    f_tab = np.concatenate([ct, st], axis=1).astype(np.float32)
    g_tab = (np.concatenate([cg, -sg], axis=1) * scale).astype(np.float32)
    return jnp.asarray(f_tab).astype(bf16), jnp.asarray(g_tab).astype(bf16)


def _fourier_kernel(x_ref, cs_ref, f_ref, o_ref, z_ref, *, t_len):
    x = x_ref[...]
    for g in range(NG):
        cols = slice(g * DG, (g + 1) * DG)
        pq = jnp.dot(x[:, cols], cs_ref[...], preferred_element_type=f32)
        z_ref[0:t_len, cols] = pq[:, :DG].astype(bf16)
        z_ref[t_len:2 * t_len, cols] = pq[:, DG:].astype(bf16)
    o_ref[...] = jnp.dot(f_ref[...], z_ref[...], preferred_element_type=f32).astype(bf16)


def _fourier_call(proj, tables, t_len, row_blk0, prev_out):
    f_tab, g_tab = tables
    kern = functools.partial(_fourier_kernel, t_len=t_len)
    in_specs = [
        pl.BlockSpec((t_len, DF), lambda b: (row_blk0 + b, COL_F // DF)),
        pl.BlockSpec((DG, 2 * DG), lambda b: (0, 0)),
        pl.BlockSpec((t_len, 2 * t_len), lambda b: (0, 0), pipeline_mode=pl.Buffered(1)),
    ]
    args = [proj, g_tab, f_tab]
    aliases = {}
    if prev_out is not None:
        in_specs.append(pl.BlockSpec(memory_space=pl.ANY))
        args.append(prev_out)
        aliases = {3: 0}

        def body(x_ref, cs_ref, f_ref, _, o_ref, z_ref):
            kern(x_ref, cs_ref, f_ref, o_ref, z_ref)
    else:
        body = kern
    return pl.pallas_call(
        body,
        grid=(B,),
        in_specs=in_specs,
        out_specs=pl.BlockSpec((t_len, DF), lambda b: (row_blk0 + b, 0)),
        out_shape=jax.ShapeDtypeStruct((R, DF), bf16),
        scratch_shapes=[pltpu.VMEM((2 * t_len, DF), bf16)],
        input_output_aliases=aliases,
        compiler_params=_cparams("arbitrary"),
        name=f"fourier_{t_len}",
    )(*args)


def _fourier(proj, tables_lat, tables_ctx, with_ctx):
    y = _fourier_call(proj, tables_lat, T, 0, None)
    return _fourier_call(proj, tables_ctx, T_CTX, R_LAT // T_CTX, y) if with_ctx else y


MG_TM = 256
MG_TC = 512


def _merge_kernel(hml_ref, hmc_ref, yf_ref, cb_ref, cc_ref, cx_ref, wc_ref, gm_ref, wpm_ref, wpf_ref, wpc_ref,
                  wo_ref, h_ref, g1_ref, o_ref):
    r0 = pl.program_id(0) * MG_TM
    is_ctx = r0 >= R_LAT
    period = jnp.where(is_ctx, T_CTX, GRID_W)
    row = lax.broadcasted_iota(jnp.int32, (MG_TM, 1), 0)
    pos = jnp.bitwise_and(row, period - 1)
    ccx = cc_ref[...].astype(f32) * cx_ref[...].astype(f32)
    x_prev = jnp.where(pos == 0, 0.0, pltpu.roll(ccx, 1, 0))
    x_next = jnp.where(pos == period - 1, 0.0, pltpu.roll(ccx, MG_TM - 1, 0))
    wc = wc_ref[...]
    uc = cb_ref[...].astype(f32) * (x_prev * wc[0:1, :] + ccx * wc[1:2, :] + x_next * wc[2:3, :])

    hm = jnp.where(is_ctx, hmc_ref[...], hml_ref[...])
    yf = yf_ref[...]
    uc = uc.astype(bf16)
    def out_part(pending, out):
        merged, cols = pending
        part = jnp.dot(merged, wo_ref[cols, :], preferred_element_type=f32)
        return part if out is None else out + part

    out, pending = None, None
    for c0 in range(0, D, MG_TC):
        cols = slice(c0, c0 + MG_TC)
        y_m = jnp.dot(hm, wpm_ref[:, cols], preferred_element_type=f32)
        y_f = jnp.dot(yf, wpf_ref[:, cols], preferred_element_type=f32)
        y_c = jnp.dot(uc, wpc_ref[:, cols], preferred_element_type=f32)
        if pending is not None:
            out = out_part(pending, out)
        merged = _sigmoid(gm_ref[:, c0:c0 + MG_TC].astype(f32)) * y_m
        merged = merged + _sigmoid(gm_ref[:, D + c0:D + c0 + MG_TC].astype(f32)) * y_f
        merged = merged + _sigmoid(gm_ref[:, 2 * D + c0:2 * D + c0 + MG_TC].astype(f32)) * y_c
        pending = (merged.astype(bf16), cols)
    out = out_part(pending, out)
    o_ref[...] = h_ref[...] + g1_ref[...] * out


def _merge(l, hm_lat, hm_ctx, yf, proj, conv_c_w, w_pm, w_pf, w_pc, w_o, h, mods, rows):
    tm = MG_TM
    n_lat, n_ctx = R_LAT // tm, R_CTX // tm

    def const(shape):
        return pl.BlockSpec((None,) + shape, lambda i: (l, 0, 0), pipeline_mode=pl.Buffered(1))

    return pl.pallas_call(
        _merge_kernel,
        grid=(rows // tm,),
        in_specs=[
            pl.BlockSpec((tm, DM), lambda i: (jnp.minimum(i, n_lat - 1), 0)),
            pl.BlockSpec((tm, DM), lambda i: (jnp.clip(i - n_lat, 0, n_ctx - 1), 0)),
            pl.BlockSpec((tm, DF), lambda i: (i, 0)),
            pl.BlockSpec((tm, DC), lambda i: (i, COL_CB // DC)),
            pl.BlockSpec((tm, DC), lambda i: (i, COL_CC // DC)),
            pl.BlockSpec((tm, DC), lambda i: (i, COL_CX // DC)),
            _layer_spec(l, (3, DC)),
            pl.BlockSpec((tm, 3 * D), lambda i: (i, COL_G // (3 * D))),
            const((DM, D)), const((DF, D)), const((DC, D)), const((D, D)),
            pl.BlockSpec((tm, D), lambda i: (i, 0)),
            _mod_spec(l, 2, tm),
        ],
        out_specs=pl.BlockSpec((tm, D), lambda i: (i, 0)),
        out_shape=jax.ShapeDtypeStruct((R, D), f32),
        compiler_params=_cparams("arbitrary"),
        name="merge",
    )(hm_lat, hm_ctx, yf, proj, proj, proj, conv_c_w, proj, w_pm, w_pf, w_pc, w_o, h, mods)


FF_TM = 256
FF_TC = 512


def _ffn_down_kernel(*refs, is_ctx, final):
    if is_ctx:
        ua_ref, ub_ref, wf_ref, wd_ref, h_ref, g2_ref, _, o_ref = refs
    elif final:
        ua_ref, up_ref, un_ref, ub_ref, wf_ref, wd_ref, h_ref, g2_ref, fw_ref, o_ref = refs
    else:
        ua_ref, up_ref, un_ref, ub_ref, wf_ref, wd_ref, h_ref, g2_ref, o_ref = refs
    r0 = pl.program_id(0) * FF_TM
    row = lax.broadcasted_iota(jnp.int32, (FF_TM, 1), 0)
    acc = None
    for c0 in range(0, DFF, FF_TC):
        cols = slice(c0, c0 + FF_TC)
        a = ua_ref[:, cols].astype(f32)
        if is_ctx:
            a_prev = jnp.where(row == 0, 0.0, pltpu.roll(a, 1, 0))
            a_next = jnp.where(row == FF_TM - 1, 0.0, pltpu.roll(a, FF_TM - 1, 0))
        else:
            top = jnp.where(r0 % T == 0, 0.0, up_ref[:, cols].astype(f32))
            bot = jnp.where((r0 + FF_TM) % T == 0, 0.0, un_ref[:, cols].astype(f32))
            a_prev = jnp.concatenate([top, a[:FF_TM - GRID_W, :]], axis=0)
            a_next = jnp.concatenate([a[GRID_W:, :], bot], axis=0)
        wf = wf_ref[:, cols]
        x = a_prev * wf[0:1, :] + a * wf[1:2, :] + a_next * wf[2:3, :]
        act = (x * _sigmoid(x) * ub_ref[:, cols].astype(f32)).astype(bf16)
        part = jnp.dot(act, wd_ref[cols, :], preferred_element_type=f32)
        acc = part if acc is None else acc + part
    h_new = h_ref[...] + g2_ref[...] * acc
    if final:
        h_new = h_new * lax.rsqrt(jnp.mean(h_new * h_new, axis=-1, keepdims=True) + EPS) * fw_ref[...]
    o_ref[...] = h_new


def _ffn_down(l, u, conv_ff_w, w_down, h, mods, prev_out, final_w=None):
    assert T_CTX == FF_TM
    tm = FF_TM
    is_ctx = prev_out is not None
    final = final_w is not None
    blk0 = R_LAT // tm if is_ctx else 0
    per = tm // GRID_W
    row_spec = lambda cb: pl.BlockSpec((tm, DFF), lambda i: (blk0 + i, cb))
    in_specs = [row_spec(0)]
    args = [u]
    if not is_ctx:
        in_specs += [
            pl.BlockSpec((GRID_W, DFF), lambda i: (jnp.maximum(i * per - 1, 0), 0)),
            pl.BlockSpec((GRID_W, DFF), lambda i: (jnp.minimum((i + 1) * per, R_LAT // GRID_W - 1), 0)),
        ]
        args += [u, u]
    in_specs += [
        row_spec(1),
        _layer_spec(l, (3, DFF)),
        pl.BlockSpec((None, DFF, D), lambda i: (l, 0, 0), pipeline_mode=pl.Buffered(1)),
        pl.BlockSpec((tm, D), lambda i: (blk0 + i, 0)),
        _mod_spec(l, 5, tm, blk0),
    ]
    args += [u, conv_ff_w, w_down, h, mods]
    aliases = {}
    if is_ctx:
        in_specs.append(pl.BlockSpec(memory_space=pl.ANY))
        args.append(prev_out)
        aliases = {len(args) - 1: 0}
    elif final:
        in_specs.append(pl.BlockSpec((1, D), lambda i: (0, 0)))
        args.append(final_w.reshape(1, D))
    return pl.pallas_call(
        functools.partial(_ffn_down_kernel, is_ctx=is_ctx, final=final),
        grid=((R_CTX if is_ctx else R_LAT) // tm,),
        in_specs=in_specs,
        out_specs=pl.BlockSpec((tm, D), lambda i: (blk0 + i, 0)),
        out_shape=jax.ShapeDtypeStruct((R_LAT if final else R, D), f32),
        input_output_aliases=aliases,
        compiler_params=_cparams("arbitrary"),
        name="ffn_down_ctx" if is_ctx else "ffn_down",
    )(*args)


def kernel(x, c, ctx, c_ctx, w_mod, b_mod, norm1_w, norm2_w, w_in, b_in, conv_q_w, conv_k_w, mlstm_norm_w,
           w_pm, w_pf, w_pc, conv_c_w, w_o, w_up, conv_ff_w, w_down, final_norm_w):
    h = jnp.concatenate([x.reshape(R_LAT, D), ctx.reshape(R_CTX, D)], axis=0)
    cc8 = jnp.concatenate([c, c_ctx[None, :], jnp.zeros((8 - B - 1, D), f32)], axis=0)
    mods = _modulation(cc8, w_mod, b_mod).reshape(DEPTH, 8, 6, 1, D)

    w_main = _w_in_prep(w_in)
    b_main = jnp.concatenate([b_in[:, :REF_GATES], b_in[:, REF_Q:]], axis=1).reshape(DEPTH, 1, N_PROJ)
    w_gate = jnp.pad(w_in[:, :, REF_GATES:REF_Q], ((0, 0), (0, 0), (0, GATE_PAD - 4 * NH))).astype(bf16)
    b_gate = jnp.pad(b_in[:, REF_GATES:REF_Q], ((0, 0), (0, GATE_PAD - 4 * NH))).reshape(DEPTH, 1, GATE_PAD)
    w_pm, w_pf, w_pc, w_o, w_down = (w.astype(bf16) for w in (w_pm, w_pf, w_pc, w_o, w_down))
    norm1_w = norm1_w.reshape(DEPTH, 1, D)
    norm2_w = norm2_w.reshape(DEPTH, 1, D)
    mlstm_norm_w = mlstm_norm_w.reshape(DEPTH, 1, DM)
    tables_lat, tables_ctx = _dft_tables(T), _dft_tables(T_CTX)

    for l in range(DEPTH):
        last = l == DEPTH - 1
        rows = R_LAT if last else R
        proj, gates = _in_proj(l, h, norm1_w, mods, w_main, b_main, w_gate, b_gate)
        g4 = gates[:, :4 * NH].reshape(R, 4, NH)
        gcol = jnp.transpose(g4, (2, 0, 1))
        grow = jnp.transpose(g4, (2, 1, 0))
        k, q = _qk_conv(l, proj, conv_k_w, conv_q_w)
        hm_lat, hm_ctx = _mlstm(l, q, k, proj, gcol, grow, mlstm_norm_w)
        yf = _fourier(proj, tables_lat, tables_ctx, not last)
        h = _merge(l, hm_lat, hm_ctx, yf, proj, conv_c_w, w_pm, w_pf, w_pc, w_o, h, mods, rows)
        u = _ffn_up(l, h, norm2_w, mods, w_up, rows)
        if last:
            return _ffn_down(l, u, conv_ff_w, w_down, h, mods, None, final_norm_w).reshape(B, T, D)
        h_new = _ffn_down(l, u, conv_ff_w, w_down, h, mods, None)
        h = _ffn_down(l, u, conv_ff_w, w_down, h, mods, h_new)
```

```python
import functools

import jax
import numpy as np
import jax.numpy as jnp
from jax import lax
from jax.experimental import pallas as pl
from jax.experimental.pallas import tpu as pltpu

f32 = jnp.float32
bf16 = jnp.bfloat16

D = 2048
B = 4
T = 2048
T_CTX = 256
DEPTH = 4
GRID_W = 64
NH = 4
DH = 256
DM = NH * DH
NG = 4
DG = 128
DF = NG * DG
DC = 512
DFF = 5632
CHUNK = 128
EPS = 1e-6

R_LAT = B * T
R_CTX = B * T_CTX
R = R_LAT + R_CTX

COL_K, COL_V, COL_Q, COL_O = 0, DM, 2 * DM, 3 * DM
COL_F = 4 * DM
COL_CB = COL_F + DF
COL_CC = COL_CB + DC
COL_CX = COL_CC + DC
COL_G = COL_CX + DC
N_PROJ = COL_G + 3 * D
REF_GATES = 2 * DM
REF_Q = REF_GATES + 4 * NH
GATE_PAD = 128

VMEM_LIMIT = 56 * 1024 * 1024


def _cparams(*sem):
    return pltpu.CompilerParams(dimension_semantics=sem, vmem_limit_bytes=VMEM_LIMIT)


def _mod_row(row_start):
    return jnp.where(row_start < R_LAT, row_start // T, B)


def _mod_spec(l, comp, tm):
    return pl.BlockSpec((None, None, None, 1, D), lambda i, *_: (l, _mod_row(i * tm), comp, 0, 0))


def _layer_spec(l, shape, col_map=None):
    if col_map is None:
        return pl.BlockSpec((None,) + shape, lambda *_: (l, 0, 0))
    return pl.BlockSpec((None,) + shape, lambda *idx: (l, 0, col_map(*idx)))


def _sigmoid(x):
    return jax.nn.sigmoid(x)


def _log_sigmoid(x):
    return jnp.minimum(x, 0.0) - jnp.log1p(jnp.exp(-jnp.abs(x)))


def _mod_kernel(c_ref, w_ref, b_ref, o_ref):
    c = c_ref[...]
    s = (c * _sigmoid(c)).astype(bf16)
    o_ref[...] = jnp.dot(s, w_ref[...].astype(bf16), preferred_element_type=f32) + b_ref[...]


def _modulation(cc8, w_mod, b_mod):
    tn = 1024
    n = 6 * D
    return pl.pallas_call(
        _mod_kernel,
        grid=(DEPTH, n // tn),
        in_specs=[
            pl.BlockSpec((8, D), lambda l, j: (0, 0)),
            pl.BlockSpec((None, D, tn), lambda l, j: (l, 0, j)),
            pl.BlockSpec((None, 1, tn), lambda l, j: (l, 0, j)),
        ],
        out_specs=pl.BlockSpec((None, 8, tn), lambda l, j: (l, 0, j)),
        out_shape=jax.ShapeDtypeStruct((DEPTH, 8, n), f32),
        compiler_params=_cparams("arbitrary", "arbitrary"),
        name="modulation",
    )(cc8, w_mod, b_mod.reshape(DEPTH, 1, n))


PREP_TN = 1024
PREP_ROWS = 256
LANES = 128
GATE_COLS = REF_Q - REF_GATES


def _w_in_prep_kernel(a_ref, b_ref, o_ref, g_ref):
    j = pl.program_id(1)

    @pl.when(j == REF_GATES // PREP_TN)
    def _():
        lane = lax.broadcasted_iota(jnp.int32, (D, GATE_PAD), 1)
        g_ref[...] = jnp.where(lane < GATE_COLS, a_ref[:, 0:GATE_PAD], 0.0).astype(bf16)

    @pl.when(j < REF_GATES // PREP_TN)
    def _():
        o_ref[...] = a_ref[...].astype(bf16)

    @pl.when(j >= REF_GATES // PREP_TN)
    def _():
        for r in range(0, D, PREP_ROWS):
            cat = jnp.concatenate([a_ref[r:r + PREP_ROWS, :], b_ref[r:r + PREP_ROWS, :]], axis=1)
            o_ref[r:r + PREP_ROWS, :] = cat[:, GATE_COLS:GATE_COLS + PREP_TN].astype(bf16)


def _w_in_prep(w_in):
    return pl.pallas_call(
        _w_in_prep_kernel,
        grid=(DEPTH, N_PROJ // PREP_TN),
        in_specs=[
            pl.BlockSpec((None, D, PREP_TN), lambda l, j: (l, 0, j)),
            pl.BlockSpec((None, D, LANES), lambda l, j: (l, 0, (j + 1) * (PREP_TN // LANES))),
        ],
        out_specs=[pl.BlockSpec((None, D, PREP_TN), lambda l, j: (l, 0, j)),
                   pl.BlockSpec((None, D, GATE_PAD), lambda l, j: (l, 0, 0))],
        out_shape=[jax.ShapeDtypeStruct((DEPTH, D, N_PROJ), bf16), jax.ShapeDtypeStruct((DEPTH, D, GATE_PAD), bf16)],
        compiler_params=_cparams("arbitrary", "arbitrary"),
        name="w_in_prep",
    )(w_in, w_in)


NORM_ROWS = 128
PROJ_TM = 1024
PROJ_TN = 1024
IN_TN = 2048


def _modulated_norm(h_ref, nw_ref, sh_ref, sc_ref, xn_ref):
    for r in range(0, h_ref.shape[0], NORM_ROWS):
        x = h_ref[r:r + NORM_ROWS, :]
        ms = jnp.mean(x * x, axis=-1, keepdims=True)
        y = x * lax.rsqrt(ms + EPS) * nw_ref[...]
        xn_ref[r:r + NORM_ROWS, :] = (y * (1.0 + sc_ref[...]) + sh_ref[...]).astype(bf16)


def _in_proj_kernel(h_ref, nw_ref, sh_ref, sc_ref, w_ref, b_ref, wg_ref, bg_ref, o_ref, g_ref, xn_ref):
    @pl.when(pl.program_id(1) == 0)
    def _():
        _modulated_norm(h_ref, nw_ref, sh_ref, sc_ref, xn_ref)
        g = jnp.dot(xn_ref[...], wg_ref[...], preferred_element_type=f32) + bg_ref[...]
        lane = lax.broadcasted_iota(jnp.int32, g.shape, 1)
        g_ref[...] = jnp.where((lane // NH) % 2 == 1, _log_sigmoid(g), g)

    acc = jnp.dot(xn_ref[...], w_ref[...], preferred_element_type=f32)
    o_ref[...] = (acc + b_ref[...]).astype(o_ref.dtype)


def _in_proj(l, h, norm1_w, mods, w_main, b_main, w_gate, b_gate):
    tm, tn = PROJ_TM, IN_TN
    return pl.pallas_call(
        _in_proj_kernel,
        grid=(R // tm, N_PROJ // tn),
        in_specs=[
            pl.BlockSpec((tm, D), lambda i, j: (i, 0)),
            _layer_spec(l, (1, D)),
            _mod_spec(l, 0, tm),
            _mod_spec(l, 1, tm),
            _layer_spec(l, (D, tn), lambda i, j: j),
            _layer_spec(l, (1, tn), lambda i, j: j),
            _layer_spec(l, (D, GATE_PAD)),
            _layer_spec(l, (1, GATE_PAD)),
        ],
        out_specs=[pl.BlockSpec((tm, tn), lambda i, j: (i, j)), pl.BlockSpec((tm, GATE_PAD), lambda i, j: (i, 0))],
        out_shape=[jax.ShapeDtypeStruct((R, N_PROJ), bf16), jax.ShapeDtypeStruct((R, GATE_PAD), f32)],
        scratch_shapes=[pltpu.VMEM((tm, D), bf16)],
        compiler_params=_cparams("arbitrary", "arbitrary"),
        name="in_proj",
    )(h, norm1_w, mods, mods, w_main, b_main, w_gate, b_gate)


def _ffn_up_kernel(h_ref, nw_ref, sh_ref, sc_ref, w_ref, o_ref, xn_ref):
    @pl.when(pl.program_id(1) == 0)
    def _():
        _modulated_norm(h_ref, nw_ref, sh_ref, sc_ref, xn_ref)

    o_ref[...] = jnp.dot(xn_ref[...], w_ref[...].astype(bf16), preferred_element_type=f32).astype(o_ref.dtype)


def _ffn_up(l, h, norm2_w, mods, w_up, rows):
    tm, tn = PROJ_TM, PROJ_TN
    return pl.pallas_call(
        _ffn_up_kernel,
        grid=(rows // tm, 2 * DFF // tn),
        in_specs=[
            pl.BlockSpec((tm, D), lambda i, j: (i, 0)),
            _layer_spec(l, (1, D)),
            _mod_spec(l, 3, tm),
            _mod_spec(l, 4, tm),
            _layer_spec(l, (D, tn), lambda i, j: j),
        ],
        out_specs=pl.BlockSpec((tm, tn), lambda i, j: (i, j)),
        out_shape=jax.ShapeDtypeStruct((rows, 2 * DFF), bf16),
        scratch_shapes=[pltpu.VMEM((tm, D), bf16)],
        compiler_params=_cparams("arbitrary", "arbitrary"),
        name="ffn_up",
    )(h, norm2_w, mods, mods, w_up)


QK_TR = 256
QK_TC = DM
HALO = 16


def _qk_conv_kernel(km, kp, kn, qm, qp, qn, wk_ref, wq_ref, ko_ref, qo_ref):
    r0 = pl.program_id(0) * QK_TR
    is_ctx = r0 >= R_LAT
    is_start = jnp.logical_or(is_ctx, r0 % T == 0)
    is_end = jnp.logical_or(is_ctx, (r0 + QK_TR) % T == 0)
    row = lax.broadcasted_iota(jnp.int32, (QK_TR, 1), 0)

    def conv_silu(m_ref, p_ref, n_ref, w_ref):
        x = m_ref[...].astype(f32)
        pv = jnp.where(is_start, 0.0, p_ref[...].astype(f32)[HALO - 1:HALO, :])
        nv = jnp.where(is_end, 0.0, n_ref[...].astype(f32)[0:1, :])
        x_prev = jnp.where(row == 0, pv, pltpu.roll(x, 1, 0))
        x_next = jnp.where(row == QK_TR - 1, nv, pltpu.roll(x, QK_TR - 1, 0))
        w = w_ref[...]
        a = x_prev * w[0:1, :] + x * w[1:2, :] + x_next * w[2:3, :]
        return a * _sigmoid(a)

    ko_ref[...] = conv_silu(km, kp, kn, wk_ref).astype(bf16)
    qo_ref[...] = (conv_silu(qm, qp, qn, wq_ref) * (DH ** -0.5)).astype(bf16)


def _qk_conv(l, proj, conv_k_w, conv_q_w):
    per = QK_TR // HALO
    nblk = R // HALO

    def main(col0):
        return pl.BlockSpec((QK_TR, QK_TC), lambda i, j: (i, col0 // QK_TC + j))

    def prev(col0):
        return pl.BlockSpec((HALO, QK_TC), lambda i, j: (jnp.maximum(i * per - 1, 0), col0 // QK_TC + j))

    def nxt(col0):
        return pl.BlockSpec((HALO, QK_TC), lambda i, j: (jnp.minimum((i + 1) * per, nblk - 1), col0 // QK_TC + j))

    wspec = _layer_spec(l, (3, QK_TC), lambda i, j: j)
    ospec = pl.BlockSpec((QK_TR, QK_TC), lambda i, j: (i, j))
    return pl.pallas_call(
        _qk_conv_kernel,
        grid=(R // QK_TR, DM // QK_TC),
        in_specs=[main(COL_K), prev(COL_K), nxt(COL_K), main(COL_Q), prev(COL_Q), nxt(COL_Q), wspec, wspec],
        out_specs=[ospec, ospec],
        out_shape=[jax.ShapeDtypeStruct((R, DM), bf16)] * 2,
        compiler_params=_cparams("arbitrary", "arbitrary"),
        name="qk_conv",
    )(proj, proj, proj, proj, proj, proj, conv_k_w, conv_q_w)


HP = 2
HPW = HP * DH


def _mlstm_chunks(refs, jobs, carries, c_ref):
    q_ref, k_ref, v_ref, gc_ref, gr_ref = refs
    t_idx = lax.broadcasted_iota(jnp.int32, (CHUNK, CHUNK), 0)
    s_idx = lax.broadcasted_iota(jnp.int32, (CHUNK, CHUNK), 1)
    nt = (((1,), (1,)), ((), ()))
    tn = (((0,), (0,)), ((), ()))

    ld = []
    for hh, c, _ in jobs:
        rows = pl.ds(pl.multiple_of(c * CHUNK, CHUNK), CHUNK)
        cols = slice(hh * DH, (hh + 1) * DH)
        ld.append((q_ref[rows, cols], k_ref[rows, cols], v_ref[rows, cols], rows))
    c_old = [c_ref[i] for i in range(len(jobs))]
    s_raw = [lax.dot_general(q, k, nt, preferred_element_type=f32) for q, k, _, _ in ld]
    q_c = [jnp.dot(q, co.astype(bf16), preferred_element_type=f32) for (q, _, _, _), co in zip(ld, c_old)]

    gate = []
    for (hh, c, reverse), (n, m), (_, _, _, rows) in zip(jobs, carries, ld):
        gcol = gc_ref[hh, rows, :]
        grow = gr_ref[hh, :, rows]
        gi, gf = (2, 3) if reverse else (0, 1)
        i_col = gcol[:, gi:gi + 1]
        lf_col = gcol[:, gf:gf + 1]
        i_row = grow[gi:gi + 1, :]
        lf_row = grow[gf:gf + 1, :]
        seen = (s_idx >= t_idx) if reverse else (s_idx <= t_idx)
        seen_t = (t_idx >= s_idx) if reverse else (t_idx <= s_idx)
        b_col = jnp.sum(jnp.where(seen, lf_row, 0.0), axis=1, keepdims=True)
        b_row = jnp.sum(jnp.where(seen_t, lf_col, 0.0), axis=0, keepdims=True)
        b_end = jnp.sum(lf_row, axis=1, keepdims=True)

        a_end = b_end - b_col + i_col
        m_new = jnp.maximum(b_end + m, jnp.max(a_end, axis=0, keepdims=True))
        w_end = jnp.exp(a_end - m_new)
        keep = jnp.exp(b_end + m - m_new)

        log_d = jnp.where(seen, b_col - b_row + i_row, -jnp.inf)
        inter = b_col + m
        m_t = jnp.maximum(inter, jnp.max(log_d, axis=1, keepdims=True))
        decay = jnp.exp(log_d - m_t)
        g_inter = jnp.exp(inter - m_t)
        gate.append((m_new, w_end, keep, m_t, decay, g_inter))

    s = [sr * g[4] for sr, g in zip(s_raw, gate)]
    s_v = [jnp.dot(si.astype(bf16), v, preferred_element_type=f32) for si, (_, _, v, _) in zip(s, ld)]
    kw = [k.astype(f32) * g[1] for (_, k, _, _), g in zip(ld, gate)]
    k_v = [lax.dot_general(kwi.astype(bf16), v, tn, preferred_element_type=f32) for kwi, (_, _, v, _) in zip(kw, ld)]

    hs, new = [], []
    for i, ((q, _, _, _), (n, m), (m_new, w_end, keep, m_t, decay, g_inter)) in enumerate(zip(ld, carries, gate)):
        num = g_inter * q_c[i] + s_v[i]
        qn = jnp.sum(q.astype(f32) * n, axis=1, keepdims=True)
        den = g_inter * qn + jnp.sum(s[i], axis=1, keepdims=True)
        hs.append(num / jnp.maximum(jnp.abs(den), jnp.exp(-m_t)))
        c_ref[i] = keep * c_old[i] + k_v[i]
        new.append((keep * n + jnp.sum(kw[i], axis=0, keepdims=True), m_new))
    return hs, tuple(new)


def _mlstm_kernel(ql, qc, kl, kc, vl, vc, gcl, gcc, grl, grc, ol, oc, nw_ref,
                  outl, outc, hbl, hbc, c_ref):
    lat = (ql, kl, vl, gcl, grl)
    ctx = (qc, kc, vc, gcc, grc)
    chains = [(hh, rev) for hh in range(HP) for rev in (False, True)]

    def emit(hb_ref, o_ref, out_ref, hh, c, h, final):
        rows = pl.ds(pl.multiple_of(c * CHUNK, CHUNK), CHUNK)
        cols = slice(hh * DH, (hh + 1) * DH)
        if not final:
            hb_ref[rows, cols] = h
            return
        ht = hb_ref[rows, cols] + h
        y = ht * lax.rsqrt(jnp.mean(ht * ht, axis=-1, keepdims=True) + EPS) * nw_ref[:, cols]
        out_ref[rows, cols] = (_sigmoid(o_ref[rows, cols].astype(f32)) * y).astype(bf16)

    def scan(refs, hb_ref, o_ref, out_ref, count, carries):
        def body(final, j, carries):
            jobs = [(hh, (count - 1 - j) if rev else j, rev) for hh, rev in chains]
            hs, carries = _mlstm_chunks(refs, jobs, carries, c_ref)
            for (hh, c, _), h in zip(jobs, hs):
                emit(hb_ref, o_ref, out_ref, hh, c, h, final)
            return carries

        carries = lax.fori_loop(0, count // 2, functools.partial(body, False), carries)
        return lax.fori_loop(count // 2, count, functools.partial(body, True), carries)

    c_ref[...] = jnp.zeros(c_ref.shape, f32)
    zero = tuple((jnp.zeros((1, DH), f32), jnp.zeros((1, 1), f32)) for _ in chains)
    carries = scan(ctx, hbc, oc, outc, T_CTX // CHUNK, zero)
    scan(lat, hbl, ol, outl, T // CHUNK, carries)


def _mlstm(l, q, k, proj, gcol, grow, norm_w):
    ctx_blk = R_LAT // T_CTX

    def lat(col0):
        return pl.BlockSpec((T, HPW), lambda b, h: (b, col0 // HPW + h))

    def ctx(col0):
        return pl.BlockSpec((T_CTX, HPW), lambda b, h: (ctx_blk + b, col0 // HPW + h))

    in_specs = [
        lat(0), ctx(0), lat(0), ctx(0), lat(COL_V), ctx(COL_V),
        pl.BlockSpec((HP, T, 4), lambda b, h: (h, b, 0)),
        pl.BlockSpec((HP, T_CTX, 4), lambda b, h: (h, ctx_blk + b, 0)),
        pl.BlockSpec((HP, 4, T), lambda b, h: (h, 0, b)),
        pl.BlockSpec((HP, 4, T_CTX), lambda b, h: (h, 0, ctx_blk + b)),
        lat(COL_O), ctx(COL_O),
        _layer_spec(l, (1, HPW), lambda b, h: h),
    ]
    return pl.pallas_call(
        _mlstm_kernel,
        grid=(B, NH // HP),
        in_specs=in_specs,
        out_specs=[pl.BlockSpec((T, HPW), lambda b, h: (b, h)), pl.BlockSpec((T_CTX, HPW), lambda b, h: (b, h))],
        out_shape=[jax.ShapeDtypeStruct((R_LAT, DM), bf16), jax.ShapeDtypeStruct((R_CTX, DM), bf16)],
        scratch_shapes=[pltpu.VMEM((T, HPW), f32), pltpu.VMEM((T_CTX, HPW), f32),
                        pltpu.VMEM((2 * HP, DH, DH), f32)],
        compiler_params=_cparams("arbitrary", "arbitrary"),
        name="mlstm",
    )(q, q, k, k, proj, proj, gcol, gcol, grow, grow, proj, proj, norm_w)


def _dft_tables(t_len):
    def cs(n):
        i = np.arange(n, dtype=np.int64)
        ang = ((i[:, None] * i[None, :]) % n) * (2.0 * np.pi / n)
        return np.cos(ang), np.sin(ang)

    ct, st = cs(t_len)
    cg, sg = cs(DG)
    scale = (t_len * DG) ** -0.5
    f_tab = np.concatenate([ct, st], axis=1).astype(np.float32)
    g_tab = (np.concatenate([cg, -sg], axis=1) * scale).astype(np.float32)
    return jnp.asarray(f_tab).astype(bf16), jnp.asarray(g_tab).astype(bf16)


def _fourier_kernel(x_ref, cs_ref, f_ref, o_ref, z_ref, *, t_len):
    x = x_ref[...]
    for g in range(NG):
        cols = slice(g * DG, (g + 1) * DG)
        pq = jnp.dot(x[:, cols], cs_ref[...], preferred_element_type=f32)
        z_ref[0:t_len, cols] = pq[:, :DG].astype(bf16)
        z_ref[t_len:2 * t_len, cols] = pq[:, DG:].astype(bf16)
    o_ref[...] = jnp.dot(f_ref[...], z_ref[...], preferred_element_type=f32).astype(bf16)


def _fourier(proj, tables, t_len, row_blk0):
    f_tab, g_tab = tables
    return pl.pallas_call(
        functools.partial(_fourier_kernel, t_len=t_len),
        grid=(B,),
        in_specs=[
            pl.BlockSpec((t_len, DF), lambda b: (row_blk0 + b, COL_F // DF)),
            pl.BlockSpec((DG, 2 * DG), lambda b: (0, 0)),
            pl.BlockSpec((t_len, 2 * t_len), lambda b: (0, 0), pipeline_mode=pl.Buffered(1)),
        ],
        out_specs=pl.BlockSpec((t_len, DF), lambda b: (b, 0)),
        out_shape=jax.ShapeDtypeStruct((B * t_len, DF), bf16),
        scratch_shapes=[pltpu.VMEM((2 * t_len, DF), bf16)],
        compiler_params=_cparams("arbitrary"),
        name=f"fourier_{t_len}",
    )(proj, g_tab, f_tab)


MG_TM = 256
MG_TC = 512


def _merge_kernel(hml_ref, hmc_ref, yfl_ref, yfc_ref, cb_ref, cc_ref, cx_ref, wc_ref, gm_ref, wpm_ref, wpf_ref,
                  wpc_ref, wo_ref, h_ref, g1_ref, o_ref):
    r0 = pl.program_id(0) * MG_TM
    is_ctx = r0 >= R_LAT
    period = jnp.where(is_ctx, T_CTX, GRID_W)
    row = lax.broadcasted_iota(jnp.int32, (MG_TM, 1), 0)
    pos = jnp.bitwise_and(row, period - 1)
    ccx = cc_ref[...].astype(f32) * cx_ref[...].astype(f32)
    x_prev = jnp.where(pos == 0, 0.0, pltpu.roll(ccx, 1, 0))
    x_next = jnp.where(pos == period - 1, 0.0, pltpu.roll(ccx, MG_TM - 1, 0))
    wc = wc_ref[...]
    uc = cb_ref[...].astype(f32) * (x_prev * wc[0:1, :] + ccx * wc[1:2, :] + x_next * wc[2:3, :])

    hm = jnp.where(is_ctx, hmc_ref[...], hml_ref[...])
    yf = jnp.where(is_ctx, yfc_ref[...], yfl_ref[...])
    uc = uc.astype(bf16)
    def out_part(pending, out):
        merged, cols = pending
        part = jnp.dot(merged, wo_ref[cols, :], preferred_element_type=f32)
        return part if out is None else out + part

    out, pending = None, None
    for c0 in range(0, D, MG_TC):
        cols = slice(c0, c0 + MG_TC)
        y_m = jnp.dot(hm, wpm_ref[:, cols], preferred_element_type=f32)
        y_f = jnp.dot(yf, wpf_ref[:, cols], preferred_element_type=f32)
        y_c = jnp.dot(uc, wpc_ref[:, cols], preferred_element_type=f32)
        if pending is not None:
            out = out_part(pending, out)
        merged = _sigmoid(gm_ref[:, c0:c0 + MG_TC].astype(f32)) * y_m
        merged = merged + _sigmoid(gm_ref[:, D + c0:D + c0 + MG_TC].astype(f32)) * y_f
        merged = merged + _sigmoid(gm_ref[:, 2 * D + c0:2 * D + c0 + MG_TC].astype(f32)) * y_c
        pending = (merged.astype(bf16), cols)
    out = out_part(pending, out)
    o_ref[...] = h_ref[...] + g1_ref[...] * out


def _merge(l, hm_lat, hm_ctx, yf_lat, yf_ctx, proj, conv_c_w, w_pm, w_pf, w_pc, w_o, h, mods, rows):
    tm = MG_TM
    n_lat, n_ctx = R_LAT // tm, R_CTX // tm

    def const(shape):
        return pl.BlockSpec((None,) + shape, lambda i: (l, 0, 0), pipeline_mode=pl.Buffered(1))

    def lat(width):
        return pl.BlockSpec((tm, width), lambda i: (jnp.minimum(i, n_lat - 1), 0))

    def ctx(width):
        return pl.BlockSpec((tm, width), lambda i: (jnp.clip(i - n_lat, 0, n_ctx - 1), 0))

    return pl.pallas_call(
        _merge_kernel,
        grid=(rows // tm,),
        in_specs=[
            lat(DM), ctx(DM), lat(DF), ctx(DF),
            pl.BlockSpec((tm, DC), lambda i: (i, COL_CB // DC)),
            pl.BlockSpec((tm, DC), lambda i: (i, COL_CC // DC)),
            pl.BlockSpec((tm, DC), lambda i: (i, COL_CX // DC)),
            _layer_spec(l, (3, DC)),
            pl.BlockSpec((tm, 3 * D), lambda i: (i, COL_G // (3 * D))),
            const((DM, D)), const((DF, D)), const((DC, D)), const((D, D)),
            pl.BlockSpec((tm, D), lambda i: (i, 0)),
            _mod_spec(l, 2, tm),
        ],
        out_specs=pl.BlockSpec((tm, D), lambda i: (i, 0)),
        out_shape=jax.ShapeDtypeStruct((rows, D), f32),
        compiler_params=_cparams("arbitrary"),
        name="merge",
    )(hm_lat, hm_ctx, yf_lat, yf_ctx, proj, proj, proj, conv_c_w, proj, w_pm, w_pf, w_pc, w_o, h, mods)


FF_TM = 256
FF_TC = 512


def _ffn_down_kernel(ua_ref, up_ref, un_ref, ub_ref, wf_ref, wd_ref, h_ref, g2_ref, *rest, final):
    fw_ref, o_ref = rest if final else (None,) + rest
    r0 = pl.program_id(0) * FF_TM
    row = lax.broadcasted_iota(jnp.int32, (FF_TM, 1), 0)

    def run(is_ctx):
        acc = None
        for c0 in range(0, DFF, FF_TC):
            cols = slice(c0, c0 + FF_TC)
            a = ua_ref[:, cols].astype(f32)
            if is_ctx:
                a_prev = jnp.where(row == 0, 0.0, pltpu.roll(a, 1, 0))
                a_next = jnp.where(row == FF_TM - 1, 0.0, pltpu.roll(a, FF_TM - 1, 0))
            else:
                top = jnp.where(r0 % T == 0, 0.0, up_ref[:, cols].astype(f32))
                bot = jnp.where((r0 + FF_TM) % T == 0, 0.0, un_ref[:, cols].astype(f32))
                a_prev = jnp.concatenate([top, a[:FF_TM - GRID_W, :]], axis=0)
                a_next = jnp.concatenate([a[GRID_W:, :], bot], axis=0)
            wf = wf_ref[:, cols]
            x = a_prev * wf[0:1, :] + a * wf[1:2, :] + a_next * wf[2:3, :]
            act = (x * _sigmoid(x) * ub_ref[:, cols].astype(f32)).astype(bf16)
            part = jnp.dot(act, wd_ref[cols, :], preferred_element_type=f32)
            acc = part if acc is None else acc + part
        h_new = h_ref[...] + g2_ref[...] * acc
        if final:
            h_new = h_new * lax.rsqrt(jnp.mean(h_new * h_new, axis=-1, keepdims=True) + EPS) * fw_ref[...]
        o_ref[...] = h_new

    if final:
        run(False)
    else:
        pl.when(r0 >= R_LAT)(functools.partial(run, True))
        pl.when(r0 < R_LAT)(functools.partial(run, False))


def _ffn_down(l, u, conv_ff_w, w_down, h, mods, final_w=None):
    assert T_CTX == FF_TM
    tm = FF_TM
    final = final_w is not None
    rows = R_LAT if final else R
    per = tm // GRID_W
    in_specs = [
        pl.BlockSpec((tm, DFF), lambda i: (i, 0)),
        pl.BlockSpec((GRID_W, DFF), lambda i: (jnp.maximum(i * per - 1, 0), 0)),
        pl.BlockSpec((GRID_W, DFF), lambda i: (jnp.minimum((i + 1) * per, R_LAT // GRID_W - 1), 0)),
        pl.BlockSpec((tm, DFF), lambda i: (i, 1)),
        _layer_spec(l, (3, DFF)),
        pl.BlockSpec((None, DFF, D), lambda i: (l, 0, 0), pipeline_mode=pl.Buffered(1)),
        pl.BlockSpec((tm, D), lambda i: (i, 0)),
        _mod_spec(l, 5, tm),
    ]
    args = [u, u, u, u, conv_ff_w, w_down, h, mods]
    if final:
        in_specs.append(pl.BlockSpec((1, D), lambda i: (0, 0)))
        args.append(final_w.reshape(1, D))
    return pl.pallas_call(
        functools.partial(_ffn_down_kernel, final=final),
        grid=(rows // tm,),
        in_specs=in_specs,
        out_specs=pl.BlockSpec((tm, D), lambda i: (i, 0)),
        out_shape=jax.ShapeDtypeStruct((rows, D), f32),
        compiler_params=_cparams("arbitrary"),
        name="ffn_down",
    )(*args)


def kernel(x, c, ctx, c_ctx, w_mod, b_mod, norm1_w, norm2_w, w_in, b_in, conv_q_w, conv_k_w, mlstm_norm_w,
           w_pm, w_pf, w_pc, conv_c_w, w_o, w_up, conv_ff_w, w_down, final_norm_w):
    h = jnp.concatenate([x.reshape(R_LAT, D), ctx.reshape(R_CTX, D)], axis=0)
    cc8 = jnp.concatenate([c, c_ctx[None, :], jnp.zeros((8 - B - 1, D), f32)], axis=0)
    mods = _modulation(cc8, w_mod, b_mod).reshape(DEPTH, 8, 6, 1, D)

    w_main, w_gate = _w_in_prep(w_in)
    b_main = jnp.concatenate([b_in[:, :REF_GATES], b_in[:, REF_Q:]], axis=1).reshape(DEPTH, 1, N_PROJ)
    b_gate = jnp.pad(b_in[:, REF_GATES:REF_Q], ((0, 0), (0, GATE_PAD - 4 * NH))).reshape(DEPTH, 1, GATE_PAD)
    w_pm, w_pf, w_pc, w_o, w_down = (w.astype(bf16) for w in (w_pm, w_pf, w_pc, w_o, w_down))
    norm1_w = norm1_w.reshape(DEPTH, 1, D)
    norm2_w = norm2_w.reshape(DEPTH, 1, D)
    mlstm_norm_w = mlstm_norm_w.reshape(DEPTH, 1, DM)
    tables_lat, tables_ctx = _dft_tables(T), _dft_tables(T_CTX)

    for l in range(DEPTH):
        last = l == DEPTH - 1
        rows = R_LAT if last else R
        proj, gates = _in_proj(l, h, norm1_w, mods, w_main, b_main, w_gate, b_gate)
        g4 = gates[:, :4 * NH].reshape(R, 4, NH)
        gcol = jnp.transpose(g4, (2, 0, 1))
        grow = jnp.transpose(g4, (2, 1, 0))
        k, q = _qk_conv(l, proj, conv_k_w, conv_q_w)
        hm_lat, hm_ctx = _mlstm(l, q, k, proj, gcol, grow, mlstm_norm_w)
        yf_lat = _fourier(proj, tables_lat, T, 0)
        yf_ctx = yf_lat if last else _fourier(proj, tables_ctx, T_CTX, R_LAT // T_CTX)
        h = _merge(l, hm_lat, hm_ctx, yf_lat, yf_ctx, proj, conv_c_w, w_pm, w_pf, w_pc, w_o, h, mods, rows)
        u = _ffn_up(l, h, norm2_w, mods, w_up, rows)
        h = _ffn_down(l, u, conv_ff_w, w_down, h, mods, final_norm_w if last else None)

    return h.reshape(B, T, D)
```

```python
import functools

import jax
import numpy as np
import jax.numpy as jnp
from jax import lax
from jax.experimental import pallas as pl
from jax.experimental.pallas import tpu as pltpu

f32 = jnp.float32
bf16 = jnp.bfloat16

D = 2048
B = 4
T = 2048
T_CTX = 256
DEPTH = 4
GRID_W = 64
NH = 4
DH = 256
DM = NH * DH
NG = 4
DG = 128
DF = NG * DG
DC = 512
DFF = 5632
CHUNK = 128
EPS = 1e-6

R_LAT = B * T
R_CTX = B * T_CTX
R = R_LAT + R_CTX

COL_K, COL_V, COL_Q, COL_O = 0, DM, 2 * DM, 3 * DM
COL_F = 4 * DM
COL_CB = COL_F + DF
COL_CC = COL_CB + DC
COL_CX = COL_CC + DC
COL_G = COL_CX + DC
N_PROJ = COL_G + 3 * D
REF_GATES = 2 * DM
REF_Q = REF_GATES + 4 * NH
GATE_PAD = 128

VMEM_LIMIT = 56 * 1024 * 1024


def _cparams(*sem):
    return pltpu.CompilerParams(dimension_semantics=sem, vmem_limit_bytes=VMEM_LIMIT)


def _mod_row(row_start):
    return jnp.where(row_start < R_LAT, row_start // T, B)


def _mod_spec(l, comp, tm):
    return pl.BlockSpec((None, None, None, 1, D), lambda i, *_: (l, _mod_row(i * tm), comp, 0, 0))


def _layer_spec(l, shape, col_map=None):
    if col_map is None:
        return pl.BlockSpec((None,) + shape, lambda *_: (l, 0, 0))
    return pl.BlockSpec((None,) + shape, lambda *idx: (l, 0, col_map(*idx)))


def _sigmoid(x):
    return jax.nn.sigmoid(x)


def _log_sigmoid(x):
    return jnp.minimum(x, 0.0) - jnp.log1p(jnp.exp(-jnp.abs(x)))


def _mod_kernel(c_ref, w_ref, b_ref, o_ref):
    c = c_ref[...]
    s = (c * _sigmoid(c)).astype(bf16)
    o_ref[...] = jnp.dot(s, w_ref[...].astype(bf16), preferred_element_type=f32) + b_ref[...]


def _modulation(cc8, w_mod, b_mod):
    tn = 1024
    n = 6 * D
    return pl.pallas_call(
        _mod_kernel,
        grid=(DEPTH, n // tn),
        in_specs=[
            pl.BlockSpec((8, D), lambda l, j: (0, 0)),
            pl.BlockSpec((None, D, tn), lambda l, j: (l, 0, j)),
            pl.BlockSpec((None, 1, tn), lambda l, j: (l, 0, j)),
        ],
        out_specs=pl.BlockSpec((None, 8, tn), lambda l, j: (l, 0, j)),
        out_shape=jax.ShapeDtypeStruct((DEPTH, 8, n), f32),
        compiler_params=_cparams("arbitrary", "arbitrary"),
        name="modulation",
    )(cc8, w_mod, b_mod.reshape(DEPTH, 1, n))


PREP_TN = 1024
PREP_ROWS = 256
LANES = 128
GATE_COLS = REF_Q - REF_GATES


def _w_in_prep_kernel(a_ref, b_ref, o_ref, g_ref):
    j = pl.program_id(1)

    @pl.when(j == REF_GATES // PREP_TN)
    def _():
        row = lax.broadcasted_iota(jnp.int32, (GATE_PAD, D), 0)
        g_ref[...] = jnp.where(row < GATE_COLS, a_ref[0:GATE_PAD, :], 0.0).astype(bf16)

    def emit(shift):
        for r in range(0, PREP_TN, PREP_ROWS):
            lo, hi = r + shift, r + shift + PREP_ROWS
            if hi <= PREP_TN:
                src = a_ref[lo:hi, :]
            else:
                src = jnp.concatenate([a_ref[lo:PREP_TN, :], b_ref[0:hi - PREP_TN, :]], axis=0)
            o_ref[:, r:r + PREP_ROWS] = src.T.astype(bf16)

    pl.when(j < REF_GATES // PREP_TN)(functools.partial(emit, 0))
    pl.when(j >= REF_GATES // PREP_TN)(functools.partial(emit, GATE_COLS))


def _w_in_prep(w_in_t):
    return pl.pallas_call(
        _w_in_prep_kernel,
        grid=(DEPTH, N_PROJ // PREP_TN),
        in_specs=[
            pl.BlockSpec((None, PREP_TN, D), lambda l, j: (l, j, 0)),
            pl.BlockSpec((None, GATE_COLS, D), lambda l, j: (l, (j + 1) * (PREP_TN // GATE_COLS), 0)),
        ],
        out_specs=[pl.BlockSpec((None, D, PREP_TN), lambda l, j: (l, 0, j)),
                   pl.BlockSpec((None, GATE_PAD, D), lambda l, j: (l, 0, 0))],
        out_shape=[jax.ShapeDtypeStruct((DEPTH, D, N_PROJ), bf16), jax.ShapeDtypeStruct((DEPTH, GATE_PAD, D), bf16)],
        compiler_params=_cparams("arbitrary", "arbitrary"),
        name="w_in_prep",
    )(w_in_t, w_in_t)


NORM_ROWS = 128
PROJ_TM = 1024
PROJ_TN = 1024
IN_TN = 2048


def _modulated_norm(h_ref, nw_ref, sh_ref, sc_ref, xn_ref):
    for r in range(0, h_ref.shape[0], NORM_ROWS):
        x = h_ref[r:r + NORM_ROWS, :]
        ms = jnp.mean(x * x, axis=-1, keepdims=True)
        y = x * lax.rsqrt(ms + EPS) * nw_ref[...]
        xn_ref[r:r + NORM_ROWS, :] = (y * (1.0 + sc_ref[...]) + sh_ref[...]).astype(bf16)


def _in_proj_kernel(h_ref, nw_ref, sh_ref, sc_ref, w_ref, b_ref, wg_ref, bg_ref, o_ref, g_ref, xn_ref):
    @pl.when(pl.program_id(1) == 0)
    def _():
        _modulated_norm(h_ref, nw_ref, sh_ref, sc_ref, xn_ref)
        g = lax.dot_general(xn_ref[...], wg_ref[...], (((1,), (1,)), ((), ())), preferred_element_type=f32)
        g = g + bg_ref[...]
        lane = lax.broadcasted_iota(jnp.int32, g.shape, 1)
        g_ref[...] = jnp.where((lane // NH) % 2 == 1, _log_sigmoid(g), g)

    acc = jnp.dot(xn_ref[...], w_ref[...], preferred_element_type=f32)
    o_ref[...] = (acc + b_ref[...]).astype(o_ref.dtype)


def _in_proj(l, h, norm1_w, mods, w_main, b_main, w_gate, b_gate):
    tm, tn = PROJ_TM, IN_TN
    return pl.pallas_call(
        _in_proj_kernel,
        grid=(R // tm, N_PROJ // tn),
        in_specs=[
            pl.BlockSpec((tm, D), lambda i, j: (i, 0)),
            _layer_spec(l, (1, D)),
            _mod_spec(l, 0, tm),
            _mod_spec(l, 1, tm),
            _layer_spec(l, (D, tn), lambda i, j: j),
            _layer_spec(l, (1, tn), lambda i, j: j),
            _layer_spec(l, (GATE_PAD, D)),
            _layer_spec(l, (1, GATE_PAD)),
        ],
        out_specs=[pl.BlockSpec((tm, tn), lambda i, j: (i, j)), pl.BlockSpec((tm, GATE_PAD), lambda i, j: (i, 0))],
        out_shape=[jax.ShapeDtypeStruct((R, N_PROJ), bf16), jax.ShapeDtypeStruct((R, GATE_PAD), f32)],
        scratch_shapes=[pltpu.VMEM((tm, D), bf16)],
        compiler_params=_cparams("arbitrary", "arbitrary"),
        name="in_proj",
    )(h, norm1_w, mods, mods, w_main, b_main, w_gate, b_gate)


def _ffn_up_kernel(h_ref, nw_ref, sh_ref, sc_ref, w_ref, o_ref, xn_ref):
    @pl.when(pl.program_id(1) == 0)
    def _():
        _modulated_norm(h_ref, nw_ref, sh_ref, sc_ref, xn_ref)

    o_ref[...] = jnp.dot(xn_ref[...], w_ref[...].astype(bf16), preferred_element_type=f32).astype(o_ref.dtype)


def _ffn_up(l, h, norm2_w, mods, w_up, rows):
    tm, tn = PROJ_TM, PROJ_TN
    return pl.pallas_call(
        _ffn_up_kernel,
        grid=(rows // tm, 2 * DFF // tn),
        in_specs=[
            pl.BlockSpec((tm, D), lambda i, j: (i, 0)),
            _layer_spec(l, (1, D)),
            _mod_spec(l, 3, tm),
            _mod_spec(l, 4, tm),
            _layer_spec(l, (D, tn), lambda i, j: j),
        ],
        out_specs=pl.BlockSpec((tm, tn), lambda i, j: (i, j)),
        out_shape=jax.ShapeDtypeStruct((rows, 2 * DFF), bf16),
        scratch_shapes=[pltpu.VMEM((tm, D), bf16)],
        compiler_params=_cparams("arbitrary", "arbitrary"),
        name="ffn_up",
    )(h, norm2_w, mods, mods, w_up)


QK_TR = 256
QK_TC = DM
HALO = 16


def _qk_conv_kernel(km, kp, kn, qm, qp, qn, wk_ref, wq_ref, ko_ref, qo_ref):
    r0 = pl.program_id(0) * QK_TR
    is_ctx = r0 >= R_LAT
    is_start = jnp.logical_or(is_ctx, r0 % T == 0)
    is_end = jnp.logical_or(is_ctx, (r0 + QK_TR) % T == 0)
    row = lax.broadcasted_iota(jnp.int32, (QK_TR, 1), 0)

    def conv_silu(m_ref, p_ref, n_ref, w_ref):
        x = m_ref[...].astype(f32)
        pv = jnp.where(is_start, 0.0, p_ref[...].astype(f32)[HALO - 1:HALO, :])
        nv = jnp.where(is_end, 0.0, n_ref[...].astype(f32)[0:1, :])
        x_prev = jnp.where(row == 0, pv, pltpu.roll(x, 1, 0))
        x_next = jnp.where(row == QK_TR - 1, nv, pltpu.roll(x, QK_TR - 1, 0))
        w = w_ref[...]
        a = x_prev * w[0:1, :] + x * w[1:2, :] + x_next * w[2:3, :]
        return a * _sigmoid(a)

    ko_ref[...] = conv_silu(km, kp, kn, wk_ref).astype(bf16)
    qo_ref[...] = (conv_silu(qm, qp, qn, wq_ref) * (DH ** -0.5)).astype(bf16)


def _qk_conv(l, proj, conv_k_w, conv_q_w):
    per = QK_TR // HALO
    nblk = R // HALO

    def main(col0):
        return pl.BlockSpec((QK_TR, QK_TC), lambda i, j: (i, col0 // QK_TC + j))

    def prev(col0):
        return pl.BlockSpec((HALO, QK_TC), lambda i, j: (jnp.maximum(i * per - 1, 0), col0 // QK_TC + j))

    def nxt(col0):
        return pl.BlockSpec((HALO, QK_TC), lambda i, j: (jnp.minimum((i + 1) * per, nblk - 1), col0 // QK_TC + j))

    wspec = _layer_spec(l, (3, QK_TC), lambda i, j: j)
    ospec = pl.BlockSpec((QK_TR, QK_TC), lambda i, j: (i, j))
    return pl.pallas_call(
        _qk_conv_kernel,
        grid=(R // QK_TR, DM // QK_TC),
        in_specs=[main(COL_K), prev(COL_K), nxt(COL_K), main(COL_Q), prev(COL_Q), nxt(COL_Q), wspec, wspec],
        out_specs=[ospec, ospec],
        out_shape=[jax.ShapeDtypeStruct((R, DM), bf16)] * 2,
        compiler_params=_cparams("arbitrary", "arbitrary"),
        name="qk_conv",
    )(proj, proj, proj, proj, proj, proj, conv_k_w, conv_q_w)


HP = 2
HPW = HP * DH
N_CHUNKS = R // CHUNK
GQ = 8


def _scan_prep_kernel(g_ref, o_ref):
    lane = lax.broadcasted_iota(jnp.int32, (N_CHUNKS, CHUNK), 1)

    def scan(x, op, ident, reverse):
        d = 1
        while d < CHUNK:
            if reverse:
                shifted, ok = pltpu.roll(x, CHUNK - d, 1), lane < CHUNK - d
            else:
                shifted, ok = pltpu.roll(x, d, 1), lane >= d
            x = op(x, jnp.where(ok, shifted, ident))
            d *= 2
        return x

    zero = jnp.zeros((N_CHUNKS, CHUNK), f32)
    for h in range(NH):
        for rev in (False, True):
            i_pre = g_ref[(2 if rev else 0) * NH + h]
            log_f = g_ref[(3 if rev else 1) * NH + h]
            b = scan(log_f, jnp.add, 0.0, rev)
            ib = i_pre - b
            p = scan(ib, jnp.maximum, -jnp.inf, rev)
            last = jnp.where(lane == (0 if rev else CHUNK - 1), b, 0.0)
            b_end = jnp.broadcast_to(jnp.sum(last, axis=1, keepdims=True), b.shape)
            a_max = b_end + jnp.broadcast_to(jnp.max(ib, axis=1, keepdims=True), b.shape)
            base = (h * 2 + (1 if rev else 0)) * GQ
            for r, val in enumerate((ib, p, b, b_end, a_max, zero, zero, zero)):
                o_ref[base + r] = val


def _scan_prep(gates_t):
    return pl.pallas_call(
        _scan_prep_kernel,
        out_shape=jax.ShapeDtypeStruct((NH * 2 * GQ, N_CHUNKS, CHUNK), f32),
        compiler_params=pltpu.CompilerParams(vmem_limit_bytes=VMEM_LIMIT),
        name="scan_prep",
    )(gates_t)


def _mlstm_chunks(refs, jobs, carries, c_ref):
    q_ref, k_ref, v_ref, g_ref, chunk0 = refs
    t_idx = lax.broadcasted_iota(jnp.int32, (CHUNK, CHUNK), 0)
    s_idx = lax.broadcasted_iota(jnp.int32, (CHUNK, CHUNK), 1)
    eye = jnp.where(t_idx == s_idx, 1.0, 0.0).astype(bf16)
    ones = jnp.ones((CHUNK, CHUNK), bf16)
    n_s = len(jobs)

    def split3(x):
        hi = x.astype(bf16)
        r1 = x - hi.astype(f32)
        mid = r1.astype(bf16)
        return hi, mid, (r1 - mid.astype(f32)).astype(bf16)

    def diag3(x_row):
        return jnp.concatenate([eye * p for p in split3(x_row)], axis=1)

    ld, gv, v_aug = [], [], []
    for hh, c, reverse in jobs:
        rows = pl.ds(pl.multiple_of(c * CHUNK, CHUNK), CHUNK)
        cols = slice(hh * DH, (hh + 1) * DH)
        ld.append((q_ref[rows, cols], k_ref[rows, cols]))
        v_aug.append(jnp.concatenate([v_ref[rows, cols], ones], axis=1))
        g0 = GQ if reverse else 0
        gv.append(g_ref[hh, c + chunk0, g0:g0 + GQ, :])
    c_old = [c_ref[i] for i in range(n_s)]
    k_t = [k.T for _, k in ld]
    s_raw = [jnp.dot(q, kt, preferred_element_type=f32) for (q, _), kt in zip(ld, k_t)]
    q_a = [jnp.dot(q, co.astype(bf16), preferred_element_type=f32) for (q, _), co in zip(ld, c_old)]

    ones3 = jnp.ones((3 * CHUNK, CHUNK), bf16)
    cols_b = [jnp.dot(jnp.concatenate([diag3(g[1:2, :]), diag3(g[2:3, :])], axis=0), ones3,
                      preferred_element_type=f32) for g in gv]

    gate = []
    for (hh, c, reverse), m, g, cb in zip(jobs, carries, gv, cols_b):
        ib, bend, a_max = g[0:1, :], g[3:4, :], g[4:5, :]
        m_new = jnp.maximum(bend + m, a_max)
        keep = jnp.exp(bend + m - m_new)
        w_end = jnp.exp(bend + ib - m_new)
        p_b, b_b = cb[:CHUNK, :], cb[CHUNK:, :]
        m_b = jnp.maximum(m, p_b)
        seen = (s_idx >= t_idx) if reverse else (s_idx <= t_idx)
        decay = jnp.exp(jnp.where(seen, ib - m_b, -jnp.inf))
        g_inter = jnp.exp(m - m_b)
        floor = jnp.exp(-b_b - m_b)
        gate.append((m_new, keep, w_end, decay, g_inter, floor))

    s_bf = [(sr * g[3]).astype(bf16) for sr, g in zip(s_raw, gate)]
    s_a = [jnp.dot(si, va, preferred_element_type=f32) for si, va in zip(s_bf, v_aug)]
    kw_t = [(kt.astype(f32) * g[2]).astype(bf16) for kt, g in zip(k_t, gate)]
    upd = [jnp.dot(kw, va, preferred_element_type=f32) for kw, va in zip(kw_t, v_aug)]

    hs, new = [], []
    for i, (m_new, keep, w_end, decay, g_inter, floor) in enumerate(gate):
        num = jnp.concatenate([g_inter, g_inter], axis=1) * q_a[i][:, :DH] + s_a[i][:, :DH]
        den = g_inter * q_a[i][:, DH:] + s_a[i][:, DH:]
        inv = 1.0 / jnp.maximum(jnp.abs(den), floor)
        hs.append(num * jnp.concatenate([inv, inv], axis=1))
        c_ref[i] = jnp.concatenate([keep, keep, keep], axis=1) * c_old[i] + upd[i]
        new.append(m_new)
    return hs, tuple(new)


def _mlstm_kernel(ql, qc, kl, kc, vl, vc, gl, gc, ol, oc, nw_ref, outl, outc, hbl, hbc, c_ref):
    lat = (ql, kl, vl, gl, 0)
    ctx = (qc, kc, vc, gc, pl.program_id(0) * (T_CTX // CHUNK))
    chains = [(hh, rev) for hh in range(HP) for rev in (False, True)]

    def emit(hb_ref, o_ref, out_ref, hh, c, h, final):
        rows = pl.ds(pl.multiple_of(c * CHUNK, CHUNK), CHUNK)
        cols = slice(hh * DH, (hh + 1) * DH)
        if not final:
            hb_ref[rows, cols] = h
            return
        ht = hb_ref[rows, cols] + h
        y = ht * lax.rsqrt(jnp.mean(ht * ht, axis=-1, keepdims=True) + EPS) * nw_ref[:, cols]
        out_ref[rows, cols] = (_sigmoid(o_ref[rows, cols].astype(f32)) * y).astype(bf16)

    def scan(refs, hb_ref, o_ref, out_ref, count, carries):
        def body(final, j, carries):
            jobs = [(hh, (count - 1 - j) if rev else j, rev) for hh, rev in chains]
            hs, carries = _mlstm_chunks(refs, jobs, carries, c_ref)
            for (hh, c, _), h in zip(jobs, hs):
                emit(hb_ref, o_ref, out_ref, hh, c, h, final)
            return carries

        carries = lax.fori_loop(0, count // 2, functools.partial(body, False), carries)
        return lax.fori_loop(count // 2, count, functools.partial(body, True), carries)

    c_ref[...] = jnp.zeros(c_ref.shape, f32)
    zero = tuple(jnp.zeros((1, CHUNK), f32) for _ in chains)
    carries = scan(ctx, hbc, oc, outc, T_CTX // CHUNK, zero)
    scan(lat, hbl, ol, outl, T // CHUNK, carries)


def _mlstm(l, q, k, proj, gprep, norm_w):
    ctx_blk = R_LAT // T_CTX
    lat_chunks, ctx_chunks = T // CHUNK, R_CTX // CHUNK

    def lat(col0):
        return pl.BlockSpec((T, HPW), lambda b, h: (b, col0 // HPW + h))

    def ctx(col0):
        return pl.BlockSpec((T_CTX, HPW), lambda b, h: (ctx_blk + b, col0 // HPW + h))

    in_specs = [
        lat(0), ctx(0), lat(0), ctx(0), lat(COL_V), ctx(COL_V),
        pl.BlockSpec((HP, lat_chunks, 2 * GQ, CHUNK), lambda b, h: (h, b, 0, 0)),
        pl.BlockSpec((HP, ctx_chunks, 2 * GQ, CHUNK), lambda b, h: (h, R_LAT // CHUNK // ctx_chunks, 0, 0)),
        lat(COL_O), ctx(COL_O),
        _layer_spec(l, (1, HPW), lambda b, h: h),
    ]
    return pl.pallas_call(
        _mlstm_kernel,
        grid=(B, NH // HP),
        in_specs=in_specs,
        out_specs=[pl.BlockSpec((T, HPW), lambda b, h: (b, h)), pl.BlockSpec((T_CTX, HPW), lambda b, h: (b, h))],
        out_shape=[jax.ShapeDtypeStruct((R_LAT, DM), bf16), jax.ShapeDtypeStruct((R_CTX, DM), bf16)],
        scratch_shapes=[pltpu.VMEM((T, HPW), f32), pltpu.VMEM((T_CTX, HPW), f32),
                        pltpu.VMEM((2 * HP, DH, DH + CHUNK), f32)],
        compiler_params=_cparams("arbitrary", "arbitrary"),
        name="mlstm",
    )(q, q, k, k, proj, proj, gprep, gprep, proj, proj, norm_w)


def _dft_tables(t_len):
    def cs(n):
        i = np.arange(n, dtype=np.int64)
        ang = ((i[:, None] * i[None, :]) % n) * (2.0 * np.pi / n)
        return np.cos(ang), np.sin(ang)

    ct, st = cs(t_len)
    cg, sg = cs(DG)
    scale = (t_len * DG) ** -0.5
    f_tab = np.concatenate([ct, st], axis=1).astype(np.float32)
    g_tab = (np.concatenate([cg, -sg], axis=1) * scale).astype(np.float32)
    return jnp.asarray(f_tab).astype(bf16), jnp.asarray(g_tab).astype(bf16)


def _fourier_kernel(x_ref, cs_ref, f_ref, o_ref, z_ref, *, t_len):
    x = x_ref[...]
    for g in range(NG):
        cols = slice(g * DG, (g + 1) * DG)
        pq = jnp.dot(x[:, cols], cs_ref[...], preferred_element_type=f32)
        z_ref[0:t_len, cols] = pq[:, :DG].astype(bf16)
        z_ref[t_len:2 * t_len, cols] = pq[:, DG:].astype(bf16)
    o_ref[...] = jnp.dot(f_ref[...], z_ref[...], preferred_element_type=f32).astype(bf16)


def _fourier(proj, tables, t_len, row_blk0):
    f_tab, g_tab = tables
    return pl.pallas_call(
        functools.partial(_fourier_kernel, t_len=t_len),
        grid=(B,),
        in_specs=[
            pl.BlockSpec((t_len, DF), lambda b: (row_blk0 + b, COL_F // DF)),
            pl.BlockSpec((DG, 2 * DG), lambda b: (0, 0)),
            pl.BlockSpec((t_len, 2 * t_len), lambda b: (0, 0), pipeline_mode=pl.Buffered(1)),
        ],
        out_specs=pl.BlockSpec((t_len, DF), lambda b: (b, 0)),
        out_shape=jax.ShapeDtypeStruct((B * t_len, DF), bf16),
        scratch_shapes=[pltpu.VMEM((2 * t_len, DF), bf16)],
        compiler_params=_cparams("arbitrary"),
        name=f"fourier_{t_len}",
    )(proj, g_tab, f_tab)


MG_TM = 256
MG_TC = 512


def _merge_kernel(hml_ref, hmc_ref, yfl_ref, yfc_ref, cb_ref, cc_ref, cx_ref, wc_ref, gm_ref, wpm_ref, wpf_ref,
                  wpc_ref, wo_ref, h_ref, g1_ref, o_ref):
    r0 = pl.program_id(0) * MG_TM
    is_ctx = r0 >= R_LAT
    period = jnp.where(is_ctx, T_CTX, GRID_W)
    row = lax.broadcasted_iota(jnp.int32, (MG_TM, 1), 0)
    pos = jnp.bitwise_and(row, period - 1)
    ccx = cc_ref[...].astype(f32) * cx_ref[...].astype(f32)
    x_prev = jnp.where(pos == 0, 0.0, pltpu.roll(ccx, 1, 0))
    x_next = jnp.where(pos == period - 1, 0.0, pltpu.roll(ccx, MG_TM - 1, 0))
    wc = wc_ref[...]
    uc = cb_ref[...].astype(f32) * (x_prev * wc[0:1, :] + ccx * wc[1:2, :] + x_next * wc[2:3, :])

    hm = jnp.where(is_ctx, hmc_ref[...], hml_ref[...])
    yf = jnp.where(is_ctx, yfc_ref[...], yfl_ref[...])
    uc = uc.astype(bf16)
    def out_part(pending, out):
        merged, cols = pending
        part = jnp.dot(merged, wo_ref[cols, :], preferred_element_type=f32)
        return part if out is None else out + part

    out, pending = None, None
    for c0 in range(0, D, MG_TC):
        cols = slice(c0, c0 + MG_TC)
        y_m = jnp.dot(hm, wpm_ref[:, cols], preferred_element_type=f32)
        y_f = jnp.dot(yf, wpf_ref[:, cols], preferred_element_type=f32)
        y_c = jnp.dot(uc, wpc_ref[:, cols], preferred_element_type=f32)
        if pending is not None:
            out = out_part(pending, out)
        merged = _sigmoid(gm_ref[:, c0:c0 + MG_TC].astype(f32)) * y_m
        merged = merged + _sigmoid(gm_ref[:, D + c0:D + c0 + MG_TC].astype(f32)) * y_f
        merged = merged + _sigmoid(gm_ref[:, 2 * D + c0:2 * D + c0 + MG_TC].astype(f32)) * y_c
        pending = (merged.astype(bf16), cols)
    out = out_part(pending, out)
    o_ref[...] = h_ref[...] + g1_ref[...] * out


def _merge(l, hm_lat, hm_ctx, yf_lat, yf_ctx, proj, conv_c_w, w_pm, w_pf, w_pc, w_o, h, mods, rows):
    tm = MG_TM
    n_lat, n_ctx = R_LAT // tm, R_CTX // tm

    def const(shape):
        return pl.BlockSpec((None,) + shape, lambda i: (l, 0, 0), pipeline_mode=pl.Buffered(1))

    def lat(width):
        return pl.BlockSpec((tm, width), lambda i: (jnp.minimum(i, n_lat - 1), 0))

    def ctx(width):
        return pl.BlockSpec((tm, width), lambda i: (jnp.clip(i - n_lat, 0, n_ctx - 1), 0))

    return pl.pallas_call(
        _merge_kernel,
        grid=(rows // tm,),
        in_specs=[
            lat(DM), ctx(DM), lat(DF), ctx(DF),
            pl.BlockSpec((tm, DC), lambda i: (i, COL_CB // DC)),
            pl.BlockSpec((tm, DC), lambda i: (i, COL_CC // DC)),
            pl.BlockSpec((tm, DC), lambda i: (i, COL_CX // DC)),
            _layer_spec(l, (3, DC)),
            pl.BlockSpec((tm, 3 * D), lambda i: (i, COL_G // (3 * D))),
            const((DM, D)), const((DF, D)), const((DC, D)), const((D, D)),
            pl.BlockSpec((tm, D), lambda i: (i, 0)),
            _mod_spec(l, 2, tm),
        ],
        out_specs=pl.BlockSpec((tm, D), lambda i: (i, 0)),
        out_shape=jax.ShapeDtypeStruct((rows, D), f32),
        compiler_params=_cparams("arbitrary"),
        name="merge",
    )(hm_lat, hm_ctx, yf_lat, yf_ctx, proj, proj, proj, conv_c_w, proj, w_pm, w_pf, w_pc, w_o, h, mods)


FF_TM = 256
FF_TC = 512


def _ffn_down_kernel(ua_ref, up_ref, un_ref, ub_ref, wf_ref, wd_ref, h_ref, g2_ref, *rest, final):
    fw_ref, o_ref = rest if final else (None,) + rest
    r0 = pl.program_id(0) * FF_TM
    row = lax.broadcasted_iota(jnp.int32, (FF_TM, 1), 0)

    def run(is_ctx):
        acc = None
        for c0 in range(0, DFF, FF_TC):
            cols = slice(c0, c0 + FF_TC)
            a = ua_ref[:, cols].astype(f32)
            if is_ctx:
                a_prev = jnp.where(row == 0, 0.0, pltpu.roll(a, 1, 0))
                a_next = jnp.where(row == FF_TM - 1, 0.0, pltpu.roll(a, FF_TM - 1, 0))
            else:
                top = jnp.where(r0 % T == 0, 0.0, up_ref[:, cols].astype(f32))
                bot = jnp.where((r0 + FF_TM) % T == 0, 0.0, un_ref[:, cols].astype(f32))
                a_prev = jnp.concatenate([top, a[:FF_TM - GRID_W, :]], axis=0)
                a_next = jnp.concatenate([a[GRID_W:, :], bot], axis=0)
            wf = wf_ref[:, cols]
            x = a_prev * wf[0:1, :] + a * wf[1:2, :] + a_next * wf[2:3, :]
            act = (x * _sigmoid(x) * ub_ref[:, cols].astype(f32)).astype(bf16)
            part = jnp.dot(act, wd_ref[cols, :], preferred_element_type=f32)
            acc = part if acc is None else acc + part
        h_new = h_ref[...] + g2_ref[...] * acc
        if final:
            h_new = h_new * lax.rsqrt(jnp.mean(h_new * h_new, axis=-1, keepdims=True) + EPS) * fw_ref[...]
        o_ref[...] = h_new

    if final:
        run(False)
    else:
        pl.when(r0 >= R_LAT)(functools.partial(run, True))
        pl.when(r0 < R_LAT)(functools.partial(run, False))


def _ffn_down(l, u, conv_ff_w, w_down, h, mods, final_w=None):
    assert T_CTX == FF_TM
    tm = FF_TM
    final = final_w is not None
    rows = R_LAT if final else R
    per = tm // GRID_W
    in_specs = [
        pl.BlockSpec((tm, DFF), lambda i: (i, 0)),
        pl.BlockSpec((GRID_W, DFF), lambda i: (jnp.maximum(i * per - 1, 0), 0)),
        pl.BlockSpec((GRID_W, DFF), lambda i: (jnp.minimum((i + 1) * per, R_LAT // GRID_W - 1), 0)),
        pl.BlockSpec((tm, DFF), lambda i: (i, 1)),
        _layer_spec(l, (3, DFF)),
        pl.BlockSpec((None, DFF, D), lambda i: (l, 0, 0), pipeline_mode=pl.Buffered(1)),
        pl.BlockSpec((tm, D), lambda i: (i, 0)),
        _mod_spec(l, 5, tm),
    ]
    args = [u, u, u, u, conv_ff_w, w_down, h, mods]
    if final:
        in_specs.append(pl.BlockSpec((1, D), lambda i: (0, 0)))
        args.append(final_w.reshape(1, D))
    return pl.pallas_call(
        functools.partial(_ffn_down_kernel, final=final),
        grid=(rows // tm,),
        in_specs=in_specs,
        out_specs=pl.BlockSpec((tm, D), lambda i: (i, 0)),
        out_shape=jax.ShapeDtypeStruct((rows, D), f32),
        compiler_params=_cparams("arbitrary"),
        name="ffn_down",
    )(*args)


def kernel(x, c, ctx, c_ctx, w_mod, b_mod, norm1_w, norm2_w, w_in, b_in, conv_q_w, conv_k_w, mlstm_norm_w,
           w_pm, w_pf, w_pc, conv_c_w, w_o, w_up, conv_ff_w, w_down, final_norm_w):
    h = jnp.concatenate([x.reshape(R_LAT, D), ctx.reshape(R_CTX, D)], axis=0)
    cc8 = jnp.concatenate([c, c_ctx[None, :], jnp.zeros((8 - B - 1, D), f32)], axis=0)
    mods = _modulation(cc8, w_mod, b_mod).reshape(DEPTH, 8, 6, 1, D)

    w_main, w_gate = _w_in_prep(jnp.swapaxes(w_in, 1, 2))
    b_main = jnp.concatenate([b_in[:, :REF_GATES], b_in[:, REF_Q:]], axis=1).reshape(DEPTH, 1, N_PROJ)
    b_gate = jnp.pad(b_in[:, REF_GATES:REF_Q], ((0, 0), (0, GATE_PAD - 4 * NH))).reshape(DEPTH, 1, GATE_PAD)
    w_pm, w_pf, w_pc, w_o, w_down = (w.astype(bf16) for w in (w_pm, w_pf, w_pc, w_o, w_down))
    norm1_w = norm1_w.reshape(DEPTH, 1, D)
    norm2_w = norm2_w.reshape(DEPTH, 1, D)
    mlstm_norm_w = mlstm_norm_w.reshape(DEPTH, 1, DM)
    tables_lat, tables_ctx = _dft_tables(T), _dft_tables(T_CTX)

    for l in range(DEPTH):
        last = l == DEPTH - 1
        rows = R_LAT if last else R
        proj, gates = _in_proj(l, h, norm1_w, mods, w_main, b_main, w_gate, b_gate)
        gates_t = gates[:, :4 * NH].T.reshape(4 * NH, N_CHUNKS, CHUNK)
        gprep = _scan_prep(gates_t).reshape(NH, 2 * GQ, N_CHUNKS, CHUNK).transpose(0, 2, 1, 3)
        k, q = _qk_conv(l, proj, conv_k_w, conv_q_w)
        hm_lat, hm_ctx = _mlstm(l, q, k, proj, gprep, mlstm_norm_w)
        yf_lat = _fourier(proj, tables_lat, T, 0)
        yf_ctx = yf_lat if last else _fourier(proj, tables_ctx, T_CTX, R_LAT // T_CTX)
        h = _merge(l, hm_lat, hm_ctx, yf_lat, yf_ctx, proj, conv_c_w, w_pm, w_pf, w_pc, w_o, h, mods, rows)
        u = _ffn_up(l, h, norm2_w, mods, w_up, rows)
        h = _ffn_down(l, u, conv_ff_w, w_down, h, mods, final_norm_w if last else None)

    return h.reshape(B, T, D)
```

```python
import functools

import jax
import numpy as np
import jax.numpy as jnp
from jax import lax
from jax.experimental import pallas as pl
from jax.experimental.pallas import tpu as pltpu

f32 = jnp.float32
bf16 = jnp.bfloat16

D = 2048
B = 4
T = 2048
T_CTX = 256
DEPTH = 4
GRID_W = 64
NH = 4
DH = 256
DM = NH * DH
NG = 4
DG = 128
DF = NG * DG
DC = 512
DFF = 5632
CHUNK = 128
EPS = 1e-6

R_LAT = B * T
R_CTX = B * T_CTX
R = R_LAT + R_CTX

COL_K, COL_V, COL_Q, COL_O = 0, DM, 2 * DM, 3 * DM
COL_F = 4 * DM
COL_CB = COL_F + DF
COL_CC = COL_CB + DC
COL_CX = COL_CC + DC
COL_G = COL_CX + DC
N_PROJ = COL_G + 3 * D
REF_GATES = 2 * DM
REF_Q = REF_GATES + 4 * NH
GATE_PAD = 128

VMEM_LIMIT = 56 * 1024 * 1024


def _cparams(*sem):
    return pltpu.CompilerParams(dimension_semantics=sem, vmem_limit_bytes=VMEM_LIMIT)


def _mod_row(row_start):
    return jnp.where(row_start < R_LAT, row_start // T, B)


def _mod_spec(l, comp, tm):
    return pl.BlockSpec((None, None, None, 1, D), lambda i, *_: (l, _mod_row(i * tm), comp, 0, 0))


def _layer_spec(l, shape, col_map=None):
    if col_map is None:
        return pl.BlockSpec((None,) + shape, lambda *_: (l, 0, 0))
    return pl.BlockSpec((None,) + shape, lambda *idx: (l, 0, col_map(*idx)))


def _sigmoid(x):
    return 0.5 * jnp.tanh(0.5 * x) + 0.5


def _log_sigmoid(x):
    return jnp.minimum(x, 0.0) - jnp.log1p(jnp.exp(-jnp.abs(x)))


def _mod_kernel(c_ref, w_ref, b_ref, o_ref):
    c = c_ref[...]
    s = (c * _sigmoid(c)).astype(bf16)
    o_ref[...] = jnp.dot(s, w_ref[...].astype(bf16), preferred_element_type=f32) + b_ref[...]


def _modulation(cc8, w_mod, b_mod):
    tn = 1024
    n = 6 * D
    return pl.pallas_call(
        _mod_kernel,
        grid=(DEPTH, n // tn),
        in_specs=[
            pl.BlockSpec((8, D), lambda l, j: (0, 0)),
            pl.BlockSpec((None, D, tn), lambda l, j: (l, 0, j)),
            pl.BlockSpec((None, 1, tn), lambda l, j: (l, 0, j)),
        ],
        out_specs=pl.BlockSpec((None, 8, tn), lambda l, j: (l, 0, j)),
        out_shape=jax.ShapeDtypeStruct((DEPTH, 8, n), f32),
        compiler_params=_cparams("arbitrary", "arbitrary"),
        name="modulation",
    )(cc8, w_mod, b_mod.reshape(DEPTH, 1, n))


PREP_TN = 1024
PREP_ROWS = 256
LANES = 128
GATE_COLS = REF_Q - REF_GATES


def _w_in_prep_kernel(a_ref, b_ref, o_ref, g_ref):
    j = pl.program_id(1)

    @pl.when(j == REF_GATES // PREP_TN)
    def _():
        row = lax.broadcasted_iota(jnp.int32, (GATE_PAD, D), 0)
        g_ref[...] = jnp.where(row < GATE_COLS, a_ref[0:GATE_PAD, :], 0.0).astype(bf16)

    def emit(shift):
        for r in range(0, PREP_TN, PREP_ROWS):
            lo, hi = r + shift, r + shift + PREP_ROWS
            if hi <= PREP_TN:
                src = a_ref[lo:hi, :]
            else:
                src = jnp.concatenate([a_ref[lo:PREP_TN, :], b_ref[0:hi - PREP_TN, :]], axis=0)
            o_ref[:, r:r + PREP_ROWS] = src.T.astype(bf16)

    pl.when(j < REF_GATES // PREP_TN)(functools.partial(emit, 0))
    pl.when(j >= REF_GATES // PREP_TN)(functools.partial(emit, GATE_COLS))


def _w_in_prep(w_in_t):
    return pl.pallas_call(
        _w_in_prep_kernel,
        grid=(DEPTH, N_PROJ // PREP_TN),
        in_specs=[
            pl.BlockSpec((None, PREP_TN, D), lambda l, j: (l, j, 0)),
            pl.BlockSpec((None, GATE_COLS, D), lambda l, j: (l, (j + 1) * (PREP_TN // GATE_COLS), 0)),
        ],
        out_specs=[pl.BlockSpec((None, D, PREP_TN), lambda l, j: (l, 0, j)),
                   pl.BlockSpec((None, GATE_PAD, D), lambda l, j: (l, 0, 0))],
        out_shape=[jax.ShapeDtypeStruct((DEPTH, D, N_PROJ), bf16), jax.ShapeDtypeStruct((DEPTH, GATE_PAD, D), bf16)],
        compiler_params=_cparams("arbitrary", "arbitrary"),
        name="w_in_prep",
    )(w_in_t, w_in_t)


NORM_ROWS = 128
PROJ_TM = 1024
PROJ_TN = 1024
IN_TN = 2048


def _modulated_norm(h_ref, nw_ref, sh_ref, sc_ref, xn_ref):
    for r in range(0, h_ref.shape[0], NORM_ROWS):
        x = h_ref[r:r + NORM_ROWS, :]
        ms = jnp.mean(x * x, axis=-1, keepdims=True)
        y = x * lax.rsqrt(ms + EPS) * nw_ref[...]
        xn_ref[r:r + NORM_ROWS, :] = (y * (1.0 + sc_ref[...]) + sh_ref[...]).astype(bf16)


def _in_proj_kernel(h_ref, nw_ref, sh_ref, sc_ref, w_ref, b_ref, wg_ref, bg_ref, o_ref, g_ref, xn_ref):
    @pl.when(pl.program_id(1) == 0)
    def _():
        _modulated_norm(h_ref, nw_ref, sh_ref, sc_ref, xn_ref)
        g = lax.dot_general(xn_ref[...], wg_ref[...], (((1,), (1,)), ((), ())), preferred_element_type=f32)
        g = g + bg_ref[...]
        lane = lax.broadcasted_iota(jnp.int32, g.shape, 1)
        g_ref[...] = jnp.where((lane // NH) % 2 == 1, _log_sigmoid(g), g)

    acc = jnp.dot(xn_ref[...], w_ref[...], preferred_element_type=f32)
    o_ref[...] = (acc + b_ref[...]).astype(o_ref.dtype)


def _in_proj(l, h, norm1_w, mods, w_main, b_main, w_gate, b_gate):
    tm, tn = PROJ_TM, IN_TN
    return pl.pallas_call(
        _in_proj_kernel,
        grid=(R // tm, N_PROJ // tn),
        in_specs=[
            pl.BlockSpec((tm, D), lambda i, j: (i, 0)),
            _layer_spec(l, (1, D)),
            _mod_spec(l, 0, tm),
            _mod_spec(l, 1, tm),
            _layer_spec(l, (D, tn), lambda i, j: j),
            _layer_spec(l, (1, tn), lambda i, j: j),
            _layer_spec(l, (GATE_PAD, D)),
            _layer_spec(l, (1, GATE_PAD)),
        ],
        out_specs=[pl.BlockSpec((tm, tn), lambda i, j: (i, j)), pl.BlockSpec((tm, GATE_PAD), lambda i, j: (i, 0))],
        out_shape=[jax.ShapeDtypeStruct((R, N_PROJ), bf16), jax.ShapeDtypeStruct((R, GATE_PAD), f32)],
        scratch_shapes=[pltpu.VMEM((tm, D), bf16)],
        compiler_params=_cparams("arbitrary", "arbitrary"),
        name="in_proj",
    )(h, norm1_w, mods, mods, w_main, b_main, w_gate, b_gate)


def _ffn_up_kernel(h_ref, nw_ref, sh_ref, sc_ref, w_ref, o_ref, xn_ref):
    @pl.when(pl.program_id(1) == 0)
    def _():
        _modulated_norm(h_ref, nw_ref, sh_ref, sc_ref, xn_ref)

    o_ref[...] = jnp.dot(xn_ref[...], w_ref[...].astype(bf16), preferred_element_type=f32).astype(o_ref.dtype)


def _ffn_up(l, h, norm2_w, mods, w_up, rows):
    tm, tn = PROJ_TM, PROJ_TN
    return pl.pallas_call(
        _ffn_up_kernel,
        grid=(rows // tm, 2 * DFF // tn),
        in_specs=[
            pl.BlockSpec((tm, D), lambda i, j: (i, 0)),
            _layer_spec(l, (1, D)),
            _mod_spec(l, 3, tm),
            _mod_spec(l, 4, tm),
            _layer_spec(l, (D, tn), lambda i, j: j),
        ],
        out_specs=pl.BlockSpec((tm, tn), lambda i, j: (i, j)),
        out_shape=jax.ShapeDtypeStruct((rows, 2 * DFF), bf16),
        scratch_shapes=[pltpu.VMEM((tm, D), bf16)],
        compiler_params=_cparams("arbitrary", "arbitrary"),
        name="ffn_up",
    )(h, norm2_w, mods, mods, w_up)


QK_TR = 256
QK_TC = DM
HALO = 16


def _qk_conv_kernel(km, kp, kn, qm, qp, qn, wk_ref, wq_ref, ko_ref, qo_ref):
    r0 = pl.program_id(0) * QK_TR
    is_ctx = r0 >= R_LAT
    is_start = jnp.logical_or(is_ctx, r0 % T == 0)
    is_end = jnp.logical_or(is_ctx, (r0 + QK_TR) % T == 0)
    row = lax.broadcasted_iota(jnp.int32, (QK_TR, 1), 0)

    def conv_silu(m_ref, p_ref, n_ref, w_ref):
        x = m_ref[...].astype(f32)
        pv = jnp.where(is_start, 0.0, p_ref[...].astype(f32)[HALO - 1:HALO, :])
        nv = jnp.where(is_end, 0.0, n_ref[...].astype(f32)[0:1, :])
        x_prev = jnp.where(row == 0, pv, pltpu.roll(x, 1, 0))
        x_next = jnp.where(row == QK_TR - 1, nv, pltpu.roll(x, QK_TR - 1, 0))
        w = w_ref[...]
        a = x_prev * w[0:1, :] + x * w[1:2, :] + x_next * w[2:3, :]
        return a * _sigmoid(a)

    ko_ref[...] = conv_silu(km, kp, kn, wk_ref).astype(bf16)
    qo_ref[...] = (conv_silu(qm, qp, qn, wq_ref) * (DH ** -0.5)).astype(bf16)


def _qk_conv(l, proj, conv_k_w, conv_q_w):
    per = QK_TR // HALO
    nblk = R // HALO

    def main(col0):
        return pl.BlockSpec((QK_TR, QK_TC), lambda i, j: (i, col0 // QK_TC + j))

    def prev(col0):
        return pl.BlockSpec((HALO, QK_TC), lambda i, j: (jnp.maximum(i * per - 1, 0), col0 // QK_TC + j))

    def nxt(col0):
        return pl.BlockSpec((HALO, QK_TC), lambda i, j: (jnp.minimum((i + 1) * per, nblk - 1), col0 // QK_TC + j))

    wspec = _layer_spec(l, (3, QK_TC), lambda i, j: j)
    ospec = pl.BlockSpec((QK_TR, QK_TC), lambda i, j: (i, j))
    return pl.pallas_call(
        _qk_conv_kernel,
        grid=(R // QK_TR, DM // QK_TC),
        in_specs=[main(COL_K), prev(COL_K), nxt(COL_K), main(COL_Q), prev(COL_Q), nxt(COL_Q), wspec, wspec],
        out_specs=[ospec, ospec],
        out_shape=[jax.ShapeDtypeStruct((R, DM), bf16)] * 2,
        compiler_params=_cparams("arbitrary", "arbitrary"),
        name="qk_conv",
    )(proj, proj, proj, proj, proj, proj, conv_k_w, conv_q_w)


HP = 2
HPW = HP * DH
N_CHUNKS = R // CHUNK
GQ = 8


def _scan_prep_kernel(g_ref, o_ref):
    lane = lax.broadcasted_iota(jnp.int32, (N_CHUNKS, CHUNK), 1)

    def scan(x, op, ident, reverse):
        d = 1
        while d < CHUNK:
            if reverse:
                shifted, ok = pltpu.roll(x, CHUNK - d, 1), lane < CHUNK - d
            else:
                shifted, ok = pltpu.roll(x, d, 1), lane >= d
            x = op(x, jnp.where(ok, shifted, ident))
            d *= 2
        return x

    zero = jnp.zeros((N_CHUNKS, CHUNK), f32)
    for h in range(NH):
        for rev in (False, True):
            i_pre = g_ref[(2 if rev else 0) * NH + h]
            log_f = g_ref[(3 if rev else 1) * NH + h]
            b = scan(log_f, jnp.add, 0.0, rev)
            ib = i_pre - b
            p = scan(ib, jnp.maximum, -jnp.inf, rev)
            last = jnp.where(lane == (0 if rev else CHUNK - 1), b, 0.0)
            b_end = jnp.broadcast_to(jnp.sum(last, axis=1, keepdims=True), b.shape)
            a_max = b_end + jnp.broadcast_to(jnp.max(ib, axis=1, keepdims=True), b.shape)
            base = (h * 2 + (1 if rev else 0)) * GQ
            for r, val in enumerate((ib, p, b, b_end, a_max, zero, zero, zero)):
                o_ref[base + r] = val


def _scan_prep(gates_t):
    return pl.pallas_call(
        _scan_prep_kernel,
        out_shape=jax.ShapeDtypeStruct((NH * 2 * GQ, N_CHUNKS, CHUNK), f32),
        compiler_params=pltpu.CompilerParams(vmem_limit_bytes=VMEM_LIMIT),
        name="scan_prep",
    )(gates_t)


def _mlstm_chunks(refs, jobs, carries, c_ref):
    q_ref, k_ref, v_ref, g_ref, chunk0 = refs
    t_idx = lax.broadcasted_iota(jnp.int32, (CHUNK, CHUNK), 0)
    s_idx = lax.broadcasted_iota(jnp.int32, (CHUNK, CHUNK), 1)
    eye = jnp.where(t_idx == s_idx, 1.0, 0.0).astype(bf16)
    ones = jnp.ones((CHUNK, CHUNK), bf16)
    n_s = len(jobs)

    def split3(x):
        hi = x.astype(bf16)
        r1 = x - hi.astype(f32)
        mid = r1.astype(bf16)
        return hi, mid, (r1 - mid.astype(f32)).astype(bf16)

    def diag3(x_row):
        return jnp.concatenate([eye * p for p in split3(x_row)], axis=1)

    ld, gv, v_aug = [], [], []
    for hh, c, reverse in jobs:
        rows = pl.ds(pl.multiple_of(c * CHUNK, CHUNK), CHUNK)
        cols = slice(hh * DH, (hh + 1) * DH)
        ld.append((q_ref[rows, cols], k_ref[rows, cols]))
        v_aug.append(jnp.concatenate([v_ref[rows, cols], ones], axis=1))
        g0 = GQ if reverse else 0
        gv.append(g_ref[hh, c + chunk0, g0:g0 + GQ, :])
    c_old = [c_ref[i] for i in range(n_s)]
    k_t = [k.T for _, k in ld]
    qk = [jnp.dot(q, jnp.concatenate([kt, co.astype(bf16)], axis=1), preferred_element_type=f32)
          for (q, _), kt, co in zip(ld, k_t, c_old)]
    s_raw = [x[:, :CHUNK] for x in qk]
    q_a = [x[:, CHUNK:] for x in qk]

    ones3 = jnp.ones((3 * CHUNK, CHUNK), bf16)
    cols_b = [jnp.dot(jnp.concatenate([diag3(g[1:2, :]), diag3(g[2:3, :])], axis=0), ones3,
                      preferred_element_type=f32) for g in gv]

    gate = []
    for (hh, c, reverse), m, g, cb in zip(jobs, carries, gv, cols_b):
        ib, bend, a_max = g[0:1, :], g[3:4, :], g[4:5, :]
        m_new = jnp.maximum(bend + m, a_max)
        keep = jnp.exp(bend + m - m_new)
        w_end = jnp.exp(bend + ib - m_new)
        p_b, b_b = cb[:CHUNK, :], cb[CHUNK:, :]
        m_b = jnp.maximum(m, p_b)
        seen = (s_idx >= t_idx) if reverse else (s_idx <= t_idx)
        decay = jnp.exp(jnp.where(seen, ib - m_b, -jnp.inf))
        g_inter = jnp.exp(m - m_b)
        floor = jnp.exp(-b_b - m_b)
        gate.append((m_new, keep, w_end, decay, g_inter, floor))

    s_bf = [(sr * g[3]).astype(bf16) for sr, g in zip(s_raw, gate)]
    kw_t = [(kt.astype(f32) * g[2]).astype(bf16) for kt, g in zip(k_t, gate)]
    su = [jnp.dot(jnp.concatenate([si, kw], axis=0), va, preferred_element_type=f32)
          for si, kw, va in zip(s_bf, kw_t, v_aug)]
    s_a = [x[:CHUNK, :] for x in su]
    upd = [x[CHUNK:, :] for x in su]

    hs, new = [], []
    for i, (m_new, keep, w_end, decay, g_inter, floor) in enumerate(gate):
        num = jnp.concatenate([g_inter, g_inter], axis=1) * q_a[i][:, :DH] + s_a[i][:, :DH]
        den = g_inter * q_a[i][:, DH:] + s_a[i][:, DH:]
        inv = 1.0 / jnp.maximum(jnp.abs(den), floor)
        hs.append(num * jnp.concatenate([inv, inv], axis=1))
        c_ref[i] = jnp.concatenate([keep, keep, keep], axis=1) * c_old[i] + upd[i]
        new.append(m_new)
    return hs, tuple(new)


def _mlstm_kernel(ql, qc, kl, kc, vl, vc, gl, gc, ol, oc, nw_ref, outl, outc, hbl, hbc, c_ref):
    lat = (ql, kl, vl, gl, 0)
    ctx = (qc, kc, vc, gc, pl.program_id(0) * (T_CTX // CHUNK))
    chains = [(hh, rev) for hh in range(HP) for rev in (False, True)]

    def emit(hb_ref, o_ref, out_ref, hh, c, h, final):
        rows = pl.ds(pl.multiple_of(c * CHUNK, CHUNK), CHUNK)
        cols = slice(hh * DH, (hh + 1) * DH)
        if not final:
            hb_ref[rows, cols] = h
            return
        ht = hb_ref[rows, cols] + h
        y = ht * lax.rsqrt(jnp.mean(ht * ht, axis=-1, keepdims=True) + EPS) * nw_ref[:, cols]
        out_ref[rows, cols] = (_sigmoid(o_ref[rows, cols].astype(f32)) * y).astype(bf16)

    def scan(refs, hb_ref, o_ref, out_ref, count, carries):
        def body(final, j, carries):
            jobs = [(hh, (count - 1 - j) if rev else j, rev) for hh, rev in chains]
            hs, carries = _mlstm_chunks(refs, jobs, carries, c_ref)
            for (hh, c, _), h in zip(jobs, hs):
                emit(hb_ref, o_ref, out_ref, hh, c, h, final)
            return carries

        carries = lax.fori_loop(0, count // 2, functools.partial(body, False), carries)
        return lax.fori_loop(count // 2, count, functools.partial(body, True), carries)

    c_ref[...] = jnp.zeros(c_ref.shape, f32)
    zero = tuple(jnp.zeros((1, CHUNK), f32) for _ in chains)
    carries = scan(ctx, hbc, oc, outc, T_CTX // CHUNK, zero)
    scan(lat, hbl, ol, outl, T // CHUNK, carries)


def _mlstm(l, q, k, proj, gprep, norm_w):
    ctx_blk = R_LAT // T_CTX
    lat_chunks, ctx_chunks = T // CHUNK, R_CTX // CHUNK

    def lat(col0):
        return pl.BlockSpec((T, HPW), lambda b, h: (b, col0 // HPW + h))

    def ctx(col0):
        return pl.BlockSpec((T_CTX, HPW), lambda b, h: (ctx_blk + b, col0 // HPW + h))

    in_specs = [
        lat(0), ctx(0), lat(0), ctx(0), lat(COL_V), ctx(COL_V),
        pl.BlockSpec((HP, lat_chunks, 2 * GQ, CHUNK), lambda b, h: (h, b, 0, 0)),
        pl.BlockSpec((HP, ctx_chunks, 2 * GQ, CHUNK), lambda b, h: (h, R_LAT // CHUNK // ctx_chunks, 0, 0)),
        lat(COL_O), ctx(COL_O),
        _layer_spec(l, (1, HPW), lambda b, h: h),
    ]
    return pl.pallas_call(
        _mlstm_kernel,
        grid=(B, NH // HP),
        in_specs=in_specs,
        out_specs=[pl.BlockSpec((T, HPW), lambda b, h: (b, h)), pl.BlockSpec((T_CTX, HPW), lambda b, h: (b, h))],
        out_shape=[jax.ShapeDtypeStruct((R_LAT, DM), bf16), jax.ShapeDtypeStruct((R_CTX, DM), bf16)],
        scratch_shapes=[pltpu.VMEM((T, HPW), f32), pltpu.VMEM((T_CTX, HPW), f32),
                        pltpu.VMEM((2 * HP, DH, DH + CHUNK), f32)],
        compiler_params=_cparams("arbitrary", "arbitrary"),
        name="mlstm",
    )(q, q, k, k, proj, proj, gprep, gprep, proj, proj, norm_w)


def _dft_tables(t_len):
    def cs(n):
        i = np.arange(n, dtype=np.int64)
        ang = ((i[:, None] * i[None, :]) % n) * (2.0 * np.pi / n)
        return np.cos(ang), np.sin(ang)

    ct, st = cs(t_len)
    cg, sg = cs(DG)
    scale = (t_len * DG) ** -0.5
    f_tab = np.concatenate([ct, st], axis=1).astype(np.float32)
    g_tab = (np.concatenate([cg, -sg], axis=1) * scale).astype(np.float32)
    return jnp.asarray(f_tab).astype(bf16), jnp.asarray(g_tab).astype(bf16)


def _fourier_kernel(x_ref, cs_ref, f_ref, o_ref, z_ref, *, t_len):
    x = x_ref[...]
    for g in range(NG):
        cols = slice(g * DG, (g + 1) * DG)
        pq = jnp.dot(x[:, cols], cs_ref[...], preferred_element_type=f32)
        z_ref[0:t_len, cols] = pq[:, :DG].astype(bf16)
        z_ref[t_len:2 * t_len, cols] = pq[:, DG:].astype(bf16)
    o_ref[...] = jnp.dot(f_ref[...], z_ref[...], preferred_element_type=f32).astype(bf16)


def _fourier(proj, tables, t_len, row_blk0):
    f_tab, g_tab = tables
    return pl.pallas_call(
        functools.partial(_fourier_kernel, t_len=t_len),
        grid=(B,),
        in_specs=[
            pl.BlockSpec((t_len, DF), lambda b: (row_blk0 + b, COL_F // DF)),
            pl.BlockSpec((DG, 2 * DG), lambda b: (0, 0)),
            pl.BlockSpec((t_len, 2 * t_len), lambda b: (0, 0), pipeline_mode=pl.Buffered(1)),
        ],
        out_specs=pl.BlockSpec((t_len, DF), lambda b: (b, 0)),
        out_shape=jax.ShapeDtypeStruct((B * t_len, DF), bf16),
        scratch_shapes=[pltpu.VMEM((2 * t_len, DF), bf16)],
        compiler_params=_cparams("arbitrary"),
        name=f"fourier_{t_len}",
    )(proj, g_tab, f_tab)


MG_TM = 256
MG_TC = 512


def _merge_kernel(hml_ref, hmc_ref, yfl_ref, yfc_ref, cb_ref, cc_ref, cx_ref, wc_ref, gm_ref, wpm_ref, wpf_ref,
                  wpc_ref, wo_ref, h_ref, g1_ref, o_ref):
    r0 = pl.program_id(0) * MG_TM
    is_ctx = r0 >= R_LAT
    period = jnp.where(is_ctx, T_CTX, GRID_W)
    row = lax.broadcasted_iota(jnp.int32, (MG_TM, 1), 0)
    pos = jnp.bitwise_and(row, period - 1)
    ccx = cc_ref[...].astype(f32) * cx_ref[...].astype(f32)
    x_prev = jnp.where(pos == 0, 0.0, pltpu.roll(ccx, 1, 0))
    x_next = jnp.where(pos == period - 1, 0.0, pltpu.roll(ccx, MG_TM - 1, 0))
    wc = wc_ref[...]
    uc = cb_ref[...].astype(f32) * (x_prev * wc[0:1, :] + ccx * wc[1:2, :] + x_next * wc[2:3, :])

    hm = jnp.where(is_ctx, hmc_ref[...], hml_ref[...])
    yf = jnp.where(is_ctx, yfc_ref[...], yfl_ref[...])
    uc = uc.astype(bf16)
    def out_part(pending, out):
        merged, cols = pending
        part = jnp.dot(merged, wo_ref[cols, :], preferred_element_type=f32)
        return part if out is None else out + part

    out, pending = None, None
    for c0 in range(0, D, MG_TC):
        cols = slice(c0, c0 + MG_TC)
        y_m = jnp.dot(hm, wpm_ref[:, cols], preferred_element_type=f32)
        y_f = jnp.dot(yf, wpf_ref[:, cols], preferred_element_type=f32)
        y_c = jnp.dot(uc, wpc_ref[:, cols], preferred_element_type=f32)
        if pending is not None:
            out = out_part(pending, out)
        merged = _sigmoid(gm_ref[:, c0:c0 + MG_TC].astype(f32)) * y_m
        merged = merged + _sigmoid(gm_ref[:, D + c0:D + c0 + MG_TC].astype(f32)) * y_f
        merged = merged + _sigmoid(gm_ref[:, 2 * D + c0:2 * D + c0 + MG_TC].astype(f32)) * y_c
        pending = (merged.astype(bf16), cols)
    out = out_part(pending, out)
    o_ref[...] = h_ref[...] + g1_ref[...] * out


def _merge(l, hm_lat, hm_ctx, yf_lat, yf_ctx, proj, conv_c_w, w_pm, w_pf, w_pc, w_o, h, mods, rows):
    tm = MG_TM
    n_lat, n_ctx = R_LAT // tm, R_CTX // tm

    def const(shape):
        return pl.BlockSpec((None,) + shape, lambda i: (l, 0, 0), pipeline_mode=pl.Buffered(1))

    def lat(width):
        return pl.BlockSpec((tm, width), lambda i: (jnp.minimum(i, n_lat - 1), 0))

    def ctx(width):
        return pl.BlockSpec((tm, width), lambda i: (jnp.clip(i - n_lat, 0, n_ctx - 1), 0))

    return pl.pallas_call(
        _merge_kernel,
        grid=(rows // tm,),
        in_specs=[
            lat(DM), ctx(DM), lat(DF), ctx(DF),
            pl.BlockSpec((tm, DC), lambda i: (i, COL_CB // DC)),
            pl.BlockSpec((tm, DC), lambda i: (i, COL_CC // DC)),
            pl.BlockSpec((tm, DC), lambda i: (i, COL_CX // DC)),
            _layer_spec(l, (3, DC)),
            pl.BlockSpec((tm, 3 * D), lambda i: (i, COL_G // (3 * D))),
            const((DM, D)), const((DF, D)), const((DC, D)), const((D, D)),
            pl.BlockSpec((tm, D), lambda i: (i, 0)),
            _mod_spec(l, 2, tm),
        ],
        out_specs=pl.BlockSpec((tm, D), lambda i: (i, 0)),
        out_shape=jax.ShapeDtypeStruct((rows, D), f32),
        compiler_params=_cparams("arbitrary"),
        name="merge",
    )(hm_lat, hm_ctx, yf_lat, yf_ctx, proj, proj, proj, conv_c_w, proj, w_pm, w_pf, w_pc, w_o, h, mods)


FF_TM = 256
FF_TC = 256


def _ffn_down_kernel(ua_ref, up_ref, un_ref, ub_ref, wf_ref, wd_ref, h_ref, g2_ref, *rest, final):
    fw_ref, o_ref = rest if final else (None,) + rest
    r0 = pl.program_id(0) * FF_TM
    row = lax.broadcasted_iota(jnp.int32, (FF_TM, 1), 0)

    def run(is_ctx):
        acc = None
        for c0 in range(0, DFF, FF_TC):
            cols = slice(c0, c0 + FF_TC)
            a = ua_ref[:, cols].astype(f32)
            if is_ctx:
                a_prev = jnp.where(row == 0, 0.0, pltpu.roll(a, 1, 0))
                a_next = jnp.where(row == FF_TM - 1, 0.0, pltpu.roll(a, FF_TM - 1, 0))
            else:
                top = jnp.where(r0 % T == 0, 0.0, up_ref[:, cols].astype(f32))
                bot = jnp.where((r0 + FF_TM) % T == 0, 0.0, un_ref[:, cols].astype(f32))
                a_prev = jnp.concatenate([top, a[:FF_TM - GRID_W, :]], axis=0)
                a_next = jnp.concatenate([a[GRID_W:, :], bot], axis=0)
            wf = wf_ref[:, cols]
            x = a_prev * wf[0:1, :] + a * wf[1:2, :] + a_next * wf[2:3, :]
            act = (x * _sigmoid(x) * ub_ref[:, cols].astype(f32)).astype(bf16)
            part = jnp.dot(act, wd_ref[cols, :], preferred_element_type=f32)
            acc = part if acc is None else acc + part
        h_new = h_ref[...] + g2_ref[...] * acc
        if final:
            h_new = h_new * lax.rsqrt(jnp.mean(h_new * h_new, axis=-1, keepdims=True) + EPS) * fw_ref[...]
        o_ref[...] = h_new

    if final:
        run(False)
    else:
        pl.when(r0 >= R_LAT)(functools.partial(run, True))
        pl.when(r0 < R_LAT)(functools.partial(run, False))


def _ffn_down(l, u, conv_ff_w, w_down, h, mods, final_w=None):
    assert T_CTX == FF_TM
    tm = FF_TM
    final = final_w is not None
    rows = R_LAT if final else R
    per = tm // GRID_W
    in_specs = [
        pl.BlockSpec((tm, DFF), lambda i: (i, 0)),
        pl.BlockSpec((GRID_W, DFF), lambda i: (jnp.maximum(i * per - 1, 0), 0)),
        pl.BlockSpec((GRID_W, DFF), lambda i: (jnp.minimum((i + 1) * per, R_LAT // GRID_W - 1), 0)),
        pl.BlockSpec((tm, DFF), lambda i: (i, 1)),
        _layer_spec(l, (3, DFF)),
        pl.BlockSpec((None, DFF, D), lambda i: (l, 0, 0), pipeline_mode=pl.Buffered(1)),
        pl.BlockSpec((tm, D), lambda i: (i, 0)),
        _mod_spec(l, 5, tm),
    ]
    args = [u, u, u, u, conv_ff_w, w_down, h, mods]
    if final:
        in_specs.append(pl.BlockSpec((1, D), lambda i: (0, 0)))
        args.append(final_w.reshape(1, D))
    return pl.pallas_call(
        functools.partial(_ffn_down_kernel, final=final),
        grid=(rows // tm,),
        in_specs=in_specs,
        out_specs=pl.BlockSpec((tm, D), lambda i: (i, 0)),
        out_shape=jax.ShapeDtypeStruct((rows, D), f32),
        compiler_params=_cparams("arbitrary"),
        name="ffn_down",
    )(*args)


def kernel(x, c, ctx, c_ctx, w_mod, b_mod, norm1_w, norm2_w, w_in, b_in, conv_q_w, conv_k_w, mlstm_norm_w,
           w_pm, w_pf, w_pc, conv_c_w, w_o, w_up, conv_ff_w, w_down, final_norm_w):
    h = jnp.concatenate([x.reshape(R_LAT, D), ctx.reshape(R_CTX, D)], axis=0)
    cc8 = jnp.concatenate([c, c_ctx[None, :], jnp.zeros((8 - B - 1, D), f32)], axis=0)
    mods = _modulation(cc8, w_mod, b_mod).reshape(DEPTH, 8, 6, 1, D)

    w_main, w_gate = _w_in_prep(jnp.swapaxes(w_in, 1, 2))
    b_main = jnp.concatenate([b_in[:, :REF_GATES], b_in[:, REF_Q:]], axis=1).reshape(DEPTH, 1, N_PROJ)
    b_gate = jnp.pad(b_in[:, REF_GATES:REF_Q], ((0, 0), (0, GATE_PAD - 4 * NH))).reshape(DEPTH, 1, GATE_PAD)
    w_pm, w_pf, w_pc, w_o, w_down = (w.astype(bf16) for w in (w_pm, w_pf, w_pc, w_o, w_down))
    norm1_w = norm1_w.reshape(DEPTH, 1, D)
    norm2_w = norm2_w.reshape(DEPTH, 1, D)
    mlstm_norm_w = mlstm_norm_w.reshape(DEPTH, 1, DM)
    tables_lat, tables_ctx = _dft_tables(T), _dft_tables(T_CTX)

    for l in range(DEPTH):
        last = l == DEPTH - 1
        rows = R_LAT if last else R
        proj, gates = _in_proj(l, h, norm1_w, mods, w_main, b_main, w_gate, b_gate)
        gates_t = gates[:, :4 * NH].T.reshape(4 * NH, N_CHUNKS, CHUNK)
        gprep = _scan_prep(gates_t).reshape(NH, 2 * GQ, N_CHUNKS, CHUNK).transpose(0, 2, 1, 3)
        k, q = _qk_conv(l, proj, conv_k_w, conv_q_w)
        hm_lat, hm_ctx = _mlstm(l, q, k, proj, gprep, mlstm_norm_w)
        yf_lat = _fourier(proj, tables_lat, T, 0)
        yf_ctx = yf_lat if last else _fourier(proj, tables_ctx, T_CTX, R_LAT // T_CTX)
        h = _merge(l, hm_lat, hm_ctx, yf_lat, yf_ctx, proj, conv_c_w, w_pm, w_pf, w_pc, w_o, h, mods, rows)
        u = _ffn_up(l, h, norm2_w, mods, w_up, rows)
        h = _ffn_down(l, u, conv_ff_w, w_down, h, mods, final_norm_w if last else None)

    return h.reshape(B, T, D)
```
